```python
import math
import jax, jax.numpy as jnp
from jax import lax
import numpy as np

D_MODEL = 1024
BATCH = 8
SEQ = 2048
DEPTH = 1

HEAD_DIM = 64
SB_HEADS = 8
DIFF_HEADS = 4
SB_WIDTH = SB_HEADS * HEAD_DIM
DIFF_QK_WIDTH = DIFF_HEADS * 2 * HEAD_DIM
DIFF_V_WIDTH = DIFF_HEADS * 2 * HEAD_DIM
N_BRANCHES = 2
IN_WIDTH = 3 * SB_WIDTH + 2 * DIFF_QK_WIDTH + DIFF_V_WIDTH + N_BRANCHES * D_MODEL
D_FF = 4 * D_MODEL
BLOCK_Q = 128
ROPE_THETA = 10000.0
LN_EPS = 1e-5
RMS_EPS = 1e-5
DEEPNORM_ALPHA = (2.0 * DEPTH) ** 0.25
DEEPNORM_BETA = (8.0 * DEPTH) ** -0.25

kernel_name = "hybrid_stickbreak_diffattn_gated_deepnorm"


def _lambda_init(layer_idx):
    return 0.8 - 0.6 * math.exp(-0.3 * layer_idx)


def _layer_norm(x, gain, bias):
    xf = x.astype(jnp.float32)
    mu = jnp.mean(xf, axis=-1, keepdims=True)
    var = jnp.mean(jnp.square(xf - mu), axis=-1, keepdims=True)
    y = (xf - mu) * lax.rsqrt(var + LN_EPS) * gain.astype(jnp.float32) + bias.astype(jnp.float32)
    return y.astype(x.dtype)


def _rms_norm(x, gain):
    xf = x.astype(jnp.float32)
    y = xf * lax.rsqrt(jnp.mean(jnp.square(xf), axis=-1, keepdims=True) + RMS_EPS) * gain.astype(jnp.float32)
    return y.astype(x.dtype)


def _rope_tables(seq, dim, dtype):
    inv_freq = ROPE_THETA ** (-jnp.arange(0, dim, 2, dtype=jnp.float32) / dim)
    ang = jnp.arange(seq, dtype=jnp.float32)[:, None] * inv_freq[None, :]
    return jnp.cos(ang).astype(dtype), jnp.sin(ang).astype(dtype)


def _rope(x, cos, sin):
    x1, x2 = jnp.split(x, 2, axis=-1)
    return jnp.concatenate([x1 * cos - x2 * sin, x2 * cos + x1 * sin], axis=-1)


def _stick_breaking_attention(q, k, v):
    s_len = q.shape[2]
    blk = min(BLOCK_Q, s_len)
    scale = q.shape[-1] ** -0.5
    outs = []
    for start in range(0, s_len, blk):
        end = start + blk
        qpos = start + jnp.arange(blk)
        kpos = jnp.arange(end)
        mask = kpos[None, :] < qpos[:, None]
        z = jnp.einsum('bhqd,bhkd->bhqk', q[:, :, start:end], k[:, :, :end]).astype(jnp.float32) * scale
        log_beta = jax.nn.log_sigmoid(z)
        log_1m_beta = jnp.where(mask, jax.nn.log_sigmoid(-z), 0.0)
        suffix = lax.cumsum(log_1m_beta, axis=3, reverse=True) - log_1m_beta
        a = jnp.where(mask, jnp.exp(log_beta + suffix), 0.0)
        outs.append(jnp.einsum('bhqk,bhkd->bhqd', a.astype(v.dtype), v[:, :, :end]))
    return jnp.concatenate(outs, axis=2)


def _differential_attention(q1, q2, k1, k2, v, lam):
    s_len = q1.shape[2]
    blk = min(BLOCK_Q, s_len)
    scale = q1.shape[-1] ** -0.5
    outs = []
    for start in range(0, s_len, blk):
        end = start + blk
        qpos = start + jnp.arange(blk)
        kpos = jnp.arange(end)
        mask = kpos[None, :] <= qpos[:, None]
        z1 = jnp.einsum('bhqd,bhkd->bhqk', q1[:, :, start:end], k1[:, :, :end]).astype(jnp.float32) * scale
        z2 = jnp.einsum('bhqd,bhkd->bhqk', q2[:, :, start:end], k2[:, :, :end]).astype(jnp.float32) * scale
        p1 = jax.nn.softmax(jnp.where(mask, z1, -jnp.inf), axis=-1)
        p2 = jax.nn.softmax(jnp.where(mask, z2, -jnp.inf), axis=-1)
        w = p1 - lam * p2
        outs.append(jnp.einsum('bhqk,bhkd->bhqd', w.astype(v.dtype), v[:, :, :end]))
    return jnp.concatenate(outs, axis=2)


def setup_inputs(seed: int = 0) -> dict:
    key = jax.random.key(seed)
    ks = jax.random.split(key, 17)
    f32 = jnp.float32

    def dense(k, shape, fan_in, gain=1.0):
        return jax.random.normal(k, shape, f32) * (fan_in ** -0.5) * gain

    x = jax.random.normal(ks[0], (BATCH, SEQ, D_MODEL), f32)
    w_in = dense(ks[1], (DEPTH, D_MODEL, IN_WIDTH), D_MODEL)
    col_gain = np.ones((IN_WIDTH,), np.float32)
    col_gain[2 * SB_WIDTH:3 * SB_WIDTH] = DEEPNORM_BETA
    v_b_start = 3 * SB_WIDTH + 2 * DIFF_QK_WIDTH
    col_gain[v_b_start:v_b_start + DIFF_V_WIDTH] = DEEPNORM_BETA
    w_in = w_in * jnp.asarray(col_gain)
    w_branch_a = dense(ks[2], (DEPTH, SB_WIDTH, D_MODEL), SB_WIDTH, DEEPNORM_BETA)
    w_branch_b = dense(ks[3], (DEPTH, DIFF_V_WIDTH, D_MODEL), DIFF_V_WIDTH, DEEPNORM_BETA)
    w_out = dense(ks[4], (DEPTH, D_MODEL, D_MODEL), D_MODEL, DEEPNORM_BETA)
    lambda_q1 = 0.1 * jax.random.normal(ks[5], (DEPTH, HEAD_DIM), f32)
    lambda_k1 = 0.1 * jax.random.normal(ks[6], (DEPTH, HEAD_DIM), f32)
    lambda_q2 = 0.1 * jax.random.normal(ks[7], (DEPTH, HEAD_DIM), f32)
    lambda_k2 = 0.1 * jax.random.normal(ks[8], (DEPTH, HEAD_DIM), f32)
    subln_gain = 1.0 + 0.02 * jax.random.normal(ks[9], (DEPTH, 2 * HEAD_DIM), f32)
    ln1_gain = 1.0 + 0.02 * jax.random.normal(ks[10], (DEPTH, D_MODEL), f32)
    ln1_bias = 0.02 * jax.random.normal(ks[11], (DEPTH, D_MODEL), f32)
    w_ff1 = dense(ks[12], (DEPTH, D_MODEL, D_FF), D_MODEL, DEEPNORM_BETA)
    w_ff2 = dense(ks[13], (DEPTH, D_FF, D_MODEL), D_FF, DEEPNORM_BETA)
    ln2_gain = 1.0 + 0.02 * jax.random.normal(ks[14], (DEPTH, D_MODEL), f32)
    ln2_bias = 0.02 * jax.random.normal(ks[15], (DEPTH, D_MODEL), f32)
    return {"x": x, "w_in": w_in, "w_branch_a": w_branch_a, "w_branch_b": w_branch_b, "w_out": w_out,
            "lambda_q1": lambda_q1, "lambda_k1": lambda_k1, "lambda_q2": lambda_q2, "lambda_k2": lambda_k2,
            "subln_gain": subln_gain, "ln1_gain": ln1_gain, "ln1_bias": ln1_bias,
            "w_ff1": w_ff1, "w_ff2": w_ff2, "ln2_gain": ln2_gain, "ln2_bias": ln2_bias}


def reference(x, w_in, w_branch_a, w_branch_b, w_out, lambda_q1, lambda_k1, lambda_q2, lambda_k2,
              subln_gain, ln1_gain, ln1_bias, w_ff1, w_ff2, ln2_gain, ln2_bias):
    b, s, _ = x.shape
    cos, sin = _rope_tables(s, HEAD_DIM, x.dtype)
    c_qa, c_ka, c_va = SB_WIDTH, 2 * SB_WIDTH, 3 * SB_WIDTH
    c_qb = c_va + DIFF_QK_WIDTH
    c_kb = c_qb + DIFF_QK_WIDTH
    c_vb = c_kb + DIFF_V_WIDTH
    c_ga = c_vb + D_MODEL
    for l in range(DEPTH):
        lam_init = _lambda_init(l)
        proj = jnp.einsum('bsd,de->bse', x, w_in[l])

        def heads(t, n_heads):
            return t.reshape(b, s, n_heads, -1).transpose(0, 2, 1, 3)

        qa = heads(proj[..., :c_qa], SB_HEADS)
        ka = heads(proj[..., c_qa:c_ka], SB_HEADS)
        va = heads(proj[..., c_ka:c_va], SB_HEADS)
        ya = _stick_breaking_attention(qa, ka, va)
        ya = ya.transpose(0, 2, 1, 3).reshape(b, s, SB_WIDTH)

        qb = proj[..., c_va:c_qb].reshape(b, s, DIFF_HEADS, 2, HEAD_DIM).transpose(0, 2, 3, 1, 4)
        kb = proj[..., c_qb:c_kb].reshape(b, s, DIFF_HEADS, 2, HEAD_DIM).transpose(0, 2, 3, 1, 4)
        vb = heads(proj[..., c_kb:c_vb], DIFF_HEADS)
        q1, q2 = _rope(qb[:, :, 0], cos, sin), _rope(qb[:, :, 1], cos, sin)
        k1, k2 = _rope(kb[:, :, 0], cos, sin), _rope(kb[:, :, 1], cos, sin)
        lam = (jnp.exp(jnp.sum(lambda_q1[l].astype(jnp.float32) * lambda_k1[l].astype(jnp.float32)))
               - jnp.exp(jnp.sum(lambda_q2[l].astype(jnp.float32) * lambda_k2[l].astype(jnp.float32)))
               + lam_init)
        yb = _differential_attention(q1, q2, k1, k2, vb, lam)
        yb = _rms_norm(yb, subln_gain[l]) * (1.0 - lam_init)
        yb = yb.transpose(0, 2, 1, 3).reshape(b, s, DIFF_V_WIDTH)

        gate_a = jax.nn.sigmoid(proj[..., c_vb:c_ga])
        gate_b = jax.nn.sigmoid(proj[..., c_ga:])
        merged = (gate_a * jnp.einsum('bse,ed->bsd', ya, w_branch_a[l])
                  + gate_b * jnp.einsum('bse,ed->bsd', yb, w_branch_b[l]))
        mix_out = jnp.einsum('bsd,de->bse', merged, w_out[l])
        x = _layer_norm(DEEPNORM_ALPHA * x + mix_out, ln1_gain[l], ln1_bias[l])

        h = jnp.square(jax.nn.relu(jnp.einsum('bsd,df->bsf', x, w_ff1[l])))
        ff_out = jnp.einsum('bsf,fd->bsd', h, w_ff2[l])
        x = _layer_norm(DEEPNORM_ALPHA * x + ff_out, ln2_gain[l], ln2_bias[l])
    return x
```

```python
import functools
import math

import numpy as np
import jax
import jax.numpy as jnp
from jax import lax
from jax.experimental import pallas as pl
from jax.experimental.pallas import tpu as pltpu

D_MODEL = 1024
HEAD_DIM = 64
SB_HEADS = 8
DIFF_HEADS = 4
SB_WIDTH = SB_HEADS * HEAD_DIM
DIFF_WIDTH = DIFF_HEADS * 2 * HEAD_DIM
QKV_WIDTH = 3 * SB_WIDTH + 3 * DIFF_WIDTH
D_FF = 4 * D_MODEL
ROPE_THETA = 10000.0
LN_EPS = 1e-5
RMS_EPS = 1e-5
DEPTH = 1
DEEPNORM_ALPHA = (2.0 * DEPTH) ** 0.25
LAMBDA_INIT = 0.8 - 0.6 * math.exp(-0.3 * 0)
QK_SCALE = HEAD_DIM ** -0.5

LANES = 128
ROW_TILE = 512
ATT_TILE = 256
FF_CHUNK = 512
VMEM_LIMIT = 56 * 1024 * 1024

F32 = jnp.float32
BF16 = jnp.bfloat16


def _dot(a, b):
    return jnp.dot(a, b, preferred_element_type=F32)


def _dot_nt(a, b):
    return lax.dot_general(a, b, (((1,), (1,)), ((), ())), preferred_element_type=F32)


def _resident(shape):
    return pl.BlockSpec(shape, lambda *_: (0,) * len(shape), pipeline_mode=pl.Buffered(1))


def _layer_norm(r, gain, bias):
    mu = jnp.mean(r, axis=-1, keepdims=True)
    c = r - mu
    var = jnp.mean(c * c, axis=-1, keepdims=True)
    return c * lax.rsqrt(var + LN_EPS) * gain + bias


def _proj_kernel(x_ref, w_ref, cos_ref, sin_ref, oa_ref, ob_ref):
    xb = x_ref[...].astype(BF16)
    lane = lax.broadcasted_iota(jnp.int32, (ROW_TILE, LANES), 1)
    first_half = (lane % HEAD_DIM) < (HEAD_DIM // 2)
    cos = cos_ref[...]
    sin = sin_ref[...]

    def rope(t):
        swapped = jnp.where(first_half, pltpu.roll(t, LANES - HEAD_DIM // 2, 1), pltpu.roll(t, HEAD_DIM // 2, 1))
        return t * cos + swapped * sin

    n_sec = SB_WIDTH // LANES
    for sec in range(6):
        acc = _dot(xb, w_ref[:, sec * SB_WIDTH:(sec + 1) * SB_WIDTH])
        out_ref = oa_ref if sec < 3 else ob_ref
        col0 = (sec % 3) * SB_WIDTH
        for g in range(n_sec):
            t = acc[:, g * LANES:(g + 1) * LANES]
            if sec in (3, 4):
                t = rope(t)
            if sec in (0, 3):
                t = t * QK_SCALE
            out_ref[:, col0 + g * LANES:col0 + (g + 1) * LANES] = t.astype(BF16)


def _project(x2d, w_qkv, cos_t, sin_t, seq):
    tokens = x2d.shape[0]
    pos_blocks = seq // ROW_TILE
    return pl.pallas_call(
        _proj_kernel,
        grid=(tokens // ROW_TILE,),
        in_specs=[
            pl.BlockSpec((ROW_TILE, D_MODEL), lambda i: (i, 0)),
            _resident((D_MODEL, QKV_WIDTH)),
            pl.BlockSpec((ROW_TILE, LANES), lambda i: (i % pos_blocks, 0)),
            pl.BlockSpec((ROW_TILE, LANES), lambda i: (i % pos_blocks, 0)),
        ],
        out_specs=[
            pl.BlockSpec((ROW_TILE, 3 * SB_WIDTH), lambda i: (i, 0)),
            pl.BlockSpec((ROW_TILE, 3 * DIFF_WIDTH), lambda i: (i, 0)),
        ],
        out_shape=[
            jax.ShapeDtypeStruct((tokens, 3 * SB_WIDTH), BF16),
            jax.ShapeDtypeStruct((tokens, 3 * DIFF_WIDTH), BF16),
        ],
        compiler_params=pltpu.CompilerParams(dimension_semantics=("arbitrary",), vmem_limit_bytes=VMEM_LIMIT),
        name="qkv_projection",
    )(x2d, w_qkv, cos_t, sin_t)


def _sb_kernel(q_ref, k_ref, v_ref, tri_ref, o_ref, acc_ref, carry_ref):
    n_tiles = q_ref.shape[1] // ATT_TILE
    tri = tri_ref[...]
    lane = lax.broadcasted_iota(jnp.int32, (ATT_TILE, LANES), 1)
    low = lane < HEAD_DIM
    row = lax.broadcasted_iota(jnp.int32, (ATT_TILE, ATT_TILE), 0)
    col = lax.broadcasted_iota(jnp.int32, (ATT_TILE, ATT_TILE), 1)
    strictly_causal = col < row

    def tile(qh, j, mask):
        start = pl.multiple_of(j * ATT_TILE, ATT_TILE)
        kt = k_ref[0, pl.ds(start, ATT_TILE), :]
        vt = v_ref[0, pl.ds(start, ATT_TILE), :]
        zero = jnp.zeros_like(vt)
        vh = (jnp.where(low, vt, zero), jnp.where(low, zero, vt))
        pv = None
        for h in range(2):
            z = _dot_nt(qh[h], kt)
            sp = jnp.maximum(z, 0.0) + jnp.log(1.0 + jnp.exp(-jnp.abs(z)))
            if mask is not None:
                sp = jnp.where(mask, sp, 0.0)
            hi = sp.astype(BF16)
            lo = (sp - hi.astype(F32)).astype(BF16)
            csum = _dot(hi, tri) + _dot(lo, tri)
            carry = carry_ref[h]
            a = jnp.exp(z - csum - jnp.concatenate([carry, carry], axis=1))
            if mask is not None:
                a = jnp.where(mask, a, 0.0)
            carry_ref[h] = carry + jnp.broadcast_to(csum[:, 0:1], (ATT_TILE, LANES))
            t = _dot(a.astype(BF16), vh[h])
            pv = t if pv is None else pv + t
        acc_ref[...] += pv

    def q_tile(qi, _):
        qstart = pl.multiple_of(qi * ATT_TILE, ATT_TILE)
        q = q_ref[0, pl.ds(qstart, ATT_TILE), :]
        zero = jnp.zeros_like(q)
        qh = (jnp.where(low, q, zero), jnp.where(low, zero, q))
        acc_ref[...] = jnp.zeros_like(acc_ref)
        carry_ref[...] = jnp.zeros_like(carry_ref)
        tile(qh, qi, strictly_causal)

        def off_diag(step, _):
            tile(qh, qi - 1 - step, None)
            return 0

        lax.fori_loop(0, qi, off_diag, 0)
        o_ref[0, pl.ds(qstart, ATT_TILE), :] = acc_ref[...].astype(BF16)
        return 0

    lax.fori_loop(0, n_tiles, q_tile, 0)


def _stick_breaking(qkv, tri, batch, seq):
    qkv = qkv.reshape(batch, seq, 3 * SB_WIDTH)
    pairs = SB_WIDTH // LANES
    blk = (1, seq, LANES)
    return pl.pallas_call(
        _sb_kernel,
        grid=(batch, pairs),
        in_specs=[
            pl.BlockSpec(blk, lambda b, p: (b, 0, p)),
            pl.BlockSpec(blk, lambda b, p: (b, 0, pairs + p)),
            pl.BlockSpec(blk, lambda b, p: (b, 0, 2 * pairs + p)),
            _resident((ATT_TILE, ATT_TILE)),
        ],
        out_specs=pl.BlockSpec(blk, lambda b, p: (b, 0, p)),
        out_shape=jax.ShapeDtypeStruct((batch, seq, SB_WIDTH), BF16),
        scratch_shapes=[
            pltpu.VMEM((ATT_TILE, LANES), F32),
            pltpu.VMEM((2, ATT_TILE, LANES), F32),
        ],
        compiler_params=pltpu.CompilerParams(dimension_semantics=("arbitrary", "arbitrary"),
                                             vmem_limit_bytes=VMEM_LIMIT),
        name="stick_breaking_attention",
    )(qkv, qkv, qkv, tri)


def _diff_kernel(q_ref, k_ref, v_ref, lq1_ref, lk1_ref, lq2_ref, lk2_ref, gain_ref, o_ref,
                 s_ref, run_ref, acc_ref):
    n_tiles = q_ref.shape[1] // ATT_TILE
    lam = (jnp.exp(jnp.sum(lq1_ref[...] * lk1_ref[...], keepdims=True))
           - jnp.exp(jnp.sum(lq2_ref[...] * lk2_ref[...], keepdims=True)) + LAMBDA_INIT)
    lane = lax.broadcasted_iota(jnp.int32, (ATT_TILE, LANES), 1)
    low = lane < HEAD_DIM
    row = lax.broadcasted_iota(jnp.int32, (ATT_TILE, ATT_TILE), 0)
    col = lax.broadcasted_iota(jnp.int32, (ATT_TILE, ATT_TILE), 1)
    causal = col <= row

    def q_tile(qi, _):
        qstart = pl.multiple_of(qi * ATT_TILE, ATT_TILE)
        q = q_ref[0, pl.ds(qstart, ATT_TILE), :]
        zero = jnp.zeros_like(q)
        qh = (jnp.where(low, q, zero), jnp.where(low, zero, q))

        def scores(j, mask):
            kt = k_ref[0, pl.ds(pl.multiple_of(j * ATT_TILE, ATT_TILE), ATT_TILE), :]
            for h in range(2):
                z = _dot_nt(qh[h], kt)
                if mask is not None:
                    z = jnp.where(mask, z, -jnp.inf)
                s_ref[h, j] = z
                run_ref[h] = jnp.maximum(run_ref[h], z)

        run_ref[...] = jnp.full_like(run_ref, -jnp.inf)
        scores(qi, causal)
        lax.fori_loop(0, qi, lambda j, c: (scores(j, None), c)[1], 0)
        m = [jnp.max(run_ref[h], axis=1, keepdims=True) for h in range(2)]

        def exps(j, c):
            for h in range(2):
                e = jnp.exp(s_ref[h, j] - m[h])
                s_ref[h, j] = e
                run_ref[h] += e
            return c

        run_ref[...] = jnp.zeros_like(run_ref)
        lax.fori_loop(0, qi + 1, exps, 0)
        r1 = 1.0 / jnp.sum(run_ref[0], axis=1, keepdims=True)
        r2 = lam / jnp.sum(run_ref[1], axis=1, keepdims=True)

        def weigh(j, c):
            w = s_ref[0, j] * r1 - s_ref[1, j] * r2
            vt = v_ref[0, pl.ds(pl.multiple_of(j * ATT_TILE, ATT_TILE), ATT_TILE), :]
            acc_ref[...] += _dot(w.astype(BF16), vt)
            return c

        acc_ref[...] = jnp.zeros_like(acc_ref)
        lax.fori_loop(0, qi + 1, weigh, 0)
        y = acc_ref[...]
        y = y * lax.rsqrt(jnp.mean(y * y, axis=-1, keepdims=True) + RMS_EPS) * gain_ref[...]
        o_ref[0, pl.ds(qstart, ATT_TILE), :] = (y * (1.0 - LAMBDA_INIT)).astype(BF16)
        return 0

    lax.fori_loop(0, n_tiles, q_tile, 0)


def _differential(qkv, lq1, lk1, lq2, lk2, gain, batch, seq):
    qkv = qkv.reshape(batch, seq, 3 * DIFF_WIDTH)
    blk = (1, seq, LANES)
    n_tiles = seq // ATT_TILE
    return pl.pallas_call(
        _diff_kernel,
        grid=(batch, DIFF_HEADS),
        in_specs=[
            pl.BlockSpec(blk, lambda b, h: (b, 0, h)),
            pl.BlockSpec(blk, lambda b, h: (b, 0, DIFF_HEADS + h)),
            pl.BlockSpec(blk, lambda b, h: (b, 0, 2 * DIFF_HEADS + h)),
            _resident((1, HEAD_DIM)), _resident((1, HEAD_DIM)), _resident((1, HEAD_DIM)), _resident((1, HEAD_DIM)),
            _resident((1, LANES)),
        ],
        out_specs=pl.BlockSpec(blk, lambda b, h: (b, 0, h)),
        out_shape=jax.ShapeDtypeStruct((batch, seq, DIFF_WIDTH), BF16),
        scratch_shapes=[
            pltpu.VMEM((2, n_tiles, ATT_TILE, ATT_TILE), F32),
            pltpu.VMEM((2, ATT_TILE, ATT_TILE), F32),
            pltpu.VMEM((ATT_TILE, LANES), F32),
        ],
        compiler_params=pltpu.CompilerParams(dimension_semantics=("arbitrary", "arbitrary"),
                                             vmem_limit_bytes=VMEM_LIMIT),
        name="differential_attention",
    )(qkv, qkv, qkv, lq1, lk1, lq2, lk2, gain)


def _merge_kernel(x_ref, ya_ref, yb_ref, wg_ref, wa_ref, wb_ref, wo_ref, gain_ref, bias_ref, o_ref, merged_ref):
    x = x_ref[...]
    xb = x.astype(BF16)
    ya = ya_ref[...]
    yb = yb_ref[...]
    for c in range(D_MODEL // FF_CHUNK):
        cols = slice(c * FF_CHUNK, (c + 1) * FF_CHUNK)
        gate_a = jax.nn.sigmoid(_dot(xb, wg_ref[:, cols]))
        gate_b = jax.nn.sigmoid(_dot(xb, wg_ref[:, D_MODEL + c * FF_CHUNK:D_MODEL + (c + 1) * FF_CHUNK]))
        merged = gate_a * _dot(ya, wa_ref[:, cols]) + gate_b * _dot(yb, wb_ref[:, cols])
        merged_ref[:, cols] = merged.astype(BF16)
    r = DEEPNORM_ALPHA * x + _dot(merged_ref[...], wo_ref[...])
    o_ref[...] = _layer_norm(r, gain_ref[...], bias_ref[...])


def _merge(x2d, ya, yb, w_gate, w_a, w_b, w_o, gain, bias):
    tokens = x2d.shape[0]
    rows = lambda width: pl.BlockSpec((ROW_TILE, width), lambda i: (i, 0))
    return pl.pallas_call(
        _merge_kernel,
        grid=(tokens // ROW_TILE,),
        in_specs=[
            rows(D_MODEL), rows(SB_WIDTH), rows(DIFF_WIDTH),
            _resident((D_MODEL, 2 * D_MODEL)), _resident((SB_WIDTH, D_MODEL)), _resident((DIFF_WIDTH, D_MODEL)),
            _resident((D_MODEL, D_MODEL)), _resident((1, D_MODEL)), _resident((1, D_MODEL)),
        ],
        out_specs=rows(D_MODEL),
        out_shape=jax.ShapeDtypeStruct((tokens, D_MODEL), F32),
        scratch_shapes=[pltpu.VMEM((ROW_TILE, D_MODEL), BF16)],
        compiler_params=pltpu.CompilerParams(dimension_semantics=("arbitrary",), vmem_limit_bytes=VMEM_LIMIT),
        name="gated_merge_layernorm",
    )(x2d, ya, yb, w_gate, w_a, w_b, w_o, gain, bias)


def _mlp_kernel(x_ref, w1_ref, w2_ref, gain_ref, bias_ref, o_ref):
    x = x_ref[...]
    xb = x.astype(BF16)
    acc = None
    for c in range(D_FF // FF_CHUNK):
        rows = slice(c * FF_CHUNK, (c + 1) * FF_CHUNK)
        h = jnp.maximum(_dot(xb, w1_ref[:, rows]), 0.0)
        t = _dot((h * h).astype(BF16), w2_ref[rows, :])
        acc = t if acc is None else acc + t
    r = DEEPNORM_ALPHA * x + acc
    o_ref[...] = _layer_norm(r, gain_ref[...], bias_ref[...])


def _mlp(x2d, w1, w2, gain, bias):
    tokens = x2d.shape[0]
    rows = pl.BlockSpec((ROW_TILE, D_MODEL), lambda i: (i, 0))
    return pl.pallas_call(
        _mlp_kernel,
        grid=(tokens // ROW_TILE,),
        in_specs=[rows, _resident((D_MODEL, D_FF)), _resident((D_FF, D_MODEL)),
                  _resident((1, D_MODEL)), _resident((1, D_MODEL))],
        out_specs=rows,
        out_shape=jax.ShapeDtypeStruct((tokens, D_MODEL), F32),
        compiler_params=pltpu.CompilerParams(dimension_semantics=("arbitrary",), vmem_limit_bytes=VMEM_LIMIT),
        name="mlp_layernorm",
    )(x2d, w1, w2, gain, bias)


def _rope_tables(seq):
    inv_freq = ROPE_THETA ** (-jnp.arange(0, HEAD_DIM, 2, dtype=F32) / HEAD_DIM)
    ang = jnp.arange(seq, dtype=F32)[:, None] * inv_freq[None, :]
    cos, sin = jnp.cos(ang), jnp.sin(ang)
    return jnp.tile(cos, (1, 4)), jnp.tile(jnp.concatenate([-sin, sin], axis=1), (1, 2))


def kernel(x, w_in, w_branch_a, w_branch_b, w_out, lambda_q1, lambda_k1, lambda_q2, lambda_k2, subln_gain,
           ln1_gain, ln1_bias, w_ff1, w_ff2, ln2_gain, ln2_bias):
    batch, seq, _ = x.shape
    assert seq % ROW_TILE == 0 and seq % ATT_TILE == 0 and w_in.shape[0] == DEPTH
    cos_t, sin_t = _rope_tables(seq)
    tri = jnp.asarray(np.tril(np.ones((ATT_TILE, ATT_TILE), np.float32)), BF16)
    x2d = x.reshape(batch * seq, D_MODEL)
    for l in range(DEPTH):
        w_l = w_in[l].astype(BF16)
        qkv_a, qkv_b = _project(x2d, w_l[:, :QKV_WIDTH], cos_t, sin_t, seq)
        ya = _stick_breaking(qkv_a, tri, batch, seq).reshape(batch * seq, SB_WIDTH)
        yb = _differential(qkv_b, lambda_q1[l][None], lambda_k1[l][None], lambda_q2[l][None], lambda_k2[l][None],
                           subln_gain[l][None], batch, seq).reshape(batch * seq, DIFF_WIDTH)
        x2d = _merge(x2d, ya, yb, w_l[:, QKV_WIDTH:], w_branch_a[l].astype(BF16), w_branch_b[l].astype(BF16),
                     w_out[l].astype(BF16), ln1_gain[l][None], ln1_bias[l][None])
        x2d = _mlp(x2d, w_ff1[l].astype(BF16), w_ff2[l].astype(BF16), ln2_gain[l][None], ln2_bias[l][None])
    return x2d.reshape(batch, seq, D_MODEL)
```

```python
import functools
import math

import numpy as np
import jax
import jax.numpy as jnp
from jax import lax
from jax.experimental import pallas as pl
from jax.experimental.pallas import tpu as pltpu

D_MODEL = 1024
HEAD_DIM = 64
SB_HEADS = 8
DIFF_HEADS = 4
SB_WIDTH = SB_HEADS * HEAD_DIM
DIFF_WIDTH = DIFF_HEADS * 2 * HEAD_DIM
QKV_WIDTH = 3 * SB_WIDTH + 3 * DIFF_WIDTH
D_FF = 4 * D_MODEL
ROPE_THETA = 10000.0
LN_EPS = 1e-5
RMS_EPS = 1e-5
DEPTH = 1
DEEPNORM_ALPHA = (2.0 * DEPTH) ** 0.25
LAMBDA_INIT = 0.8 - 0.6 * math.exp(-0.3 * 0)
QK_SCALE = HEAD_DIM ** -0.5

LANES = 128
ROW_TILE = 512
ATT_TILE = 256
FF_CHUNK = 512
VMEM_LIMIT = 56 * 1024 * 1024

F32 = jnp.float32
BF16 = jnp.bfloat16


def _dot(a, b):
    return jnp.dot(a, b, preferred_element_type=F32)


def _dot_nt(a, b):
    return lax.dot_general(a, b, (((1,), (1,)), ((), ())), preferred_element_type=F32)


def _resident(shape):
    return pl.BlockSpec(shape, lambda *_: (0,) * len(shape), pipeline_mode=pl.Buffered(1))


def _layer_norm(r, gain, bias):
    mu = jnp.mean(r, axis=-1, keepdims=True)
    c = r - mu
    var = jnp.mean(c * c, axis=-1, keepdims=True)
    return c * lax.rsqrt(var + LN_EPS) * gain + bias


def _proj_kernel(x_ref, w_ref, cos_ref, sin_ref, oa_ref, ob_ref):
    xb = x_ref[...].astype(BF16)
    lane = lax.broadcasted_iota(jnp.int32, (ROW_TILE, LANES), 1)
    first_half = (lane % HEAD_DIM) < (HEAD_DIM // 2)
    cos = cos_ref[...]
    sin = sin_ref[...]

    def rope(t):
        swapped = jnp.where(first_half, pltpu.roll(t, LANES - HEAD_DIM // 2, 1), pltpu.roll(t, HEAD_DIM // 2, 1))
        return t * cos + swapped * sin

    n_sec = SB_WIDTH // LANES
    for sec in range(6):
        acc = _dot(xb, w_ref[:, sec * SB_WIDTH:(sec + 1) * SB_WIDTH])
        out_ref = oa_ref if sec < 3 else ob_ref
        col0 = (sec % 3) * SB_WIDTH
        for g in range(n_sec):
            t = acc[:, g * LANES:(g + 1) * LANES]
            if sec in (3, 4):
                t = rope(t)
            if sec in (0, 3):
                t = t * QK_SCALE
            out_ref[:, col0 + g * LANES:col0 + (g + 1) * LANES] = t.astype(BF16)


def _project(x2d, w_qkv, cos_t, sin_t, seq):
    tokens = x2d.shape[0]
    pos_blocks = seq // ROW_TILE
    return pl.pallas_call(
        _proj_kernel,
        grid=(tokens // ROW_TILE,),
        in_specs=[
            pl.BlockSpec((ROW_TILE, D_MODEL), lambda i: (i, 0)),
            _resident((D_MODEL, QKV_WIDTH)),
            pl.BlockSpec((ROW_TILE, LANES), lambda i: (i % pos_blocks, 0)),
            pl.BlockSpec((ROW_TILE, LANES), lambda i: (i % pos_blocks, 0)),
        ],
        out_specs=[
            pl.BlockSpec((ROW_TILE, 3 * SB_WIDTH), lambda i: (i, 0)),
            pl.BlockSpec((ROW_TILE, 3 * DIFF_WIDTH), lambda i: (i, 0)),
        ],
        out_shape=[
            jax.ShapeDtypeStruct((tokens, 3 * SB_WIDTH), BF16),
            jax.ShapeDtypeStruct((tokens, 3 * DIFF_WIDTH), BF16),
        ],
        compiler_params=pltpu.CompilerParams(dimension_semantics=("arbitrary",), vmem_limit_bytes=VMEM_LIMIT),
        name="qkv_projection",
    )(x2d, w_qkv, cos_t, sin_t)


def _sb_kernel(q_ref, k_ref, v_ref, tri_ref, o_ref, vh_ref):
    n_tiles = q_ref.shape[1] // ATT_TILE
    tri = tri_ref[...]
    lane = lax.broadcasted_iota(jnp.int32, (ATT_TILE, LANES), 1)
    low = lane < HEAD_DIM
    row = lax.broadcasted_iota(jnp.int32, (ATT_TILE, ATT_TILE), 0)
    col = lax.broadcasted_iota(jnp.int32, (ATT_TILE, ATT_TILE), 1)
    strictly_causal = col < row

    def keys(ref, j):
        return ref[0, j * ATT_TILE:(j + 1) * ATT_TILE, :]

    for j in range(n_tiles):
        vt = keys(v_ref, j)
        zero = jnp.zeros_like(vt)
        vh_ref[0, j * ATT_TILE:(j + 1) * ATT_TILE, :] = jnp.where(low, vt, zero)
        vh_ref[1, j * ATT_TILE:(j + 1) * ATT_TILE, :] = jnp.where(low, zero, vt)

    tiles = [(qi, j) for qi in range(n_tiles) for j in range(qi, -1, -1)]
    qh, scored, summed = {}, {}, {}
    state = {"acc": None, "carry": [None, None]}

    def score(qi, j):
        if j == qi:
            q = keys(q_ref, qi)
            zero = jnp.zeros_like(q)
            qh[qi] = (jnp.where(low, q, zero), jnp.where(low, zero, q))
        kt = keys(k_ref, j)
        out = []
        for h in range(2):
            z = _dot_nt(qh[qi][h], kt)
            sp = jnp.maximum(z, 0.0) + jnp.log(1.0 + jnp.exp(-jnp.abs(z)))
            if j == qi:
                sp = jnp.where(strictly_causal, sp, 0.0)
            hi = sp.astype(BF16)
            out.append((z, hi, (sp - hi.astype(F32)).astype(BF16)))
        scored[qi, j] = out

    def in_tile_sums(qi, j):
        summed[qi, j] = [(z, _dot(hi, tri) + _dot(lo, tri)) for z, hi, lo in scored.pop((qi, j))]

    def value_product(qi, j):
        if j == qi:
            state["acc"], state["carry"] = None, [None, None]
        carry = state["carry"]
        for h, (z, csum) in enumerate(summed.pop((qi, j))):
            x = z - csum
            if carry[h] is not None:
                x = x - jnp.concatenate([carry[h], carry[h]], axis=1)
            a = jnp.exp(x)
            if j == qi:
                a = jnp.where(strictly_causal, a, 0.0)
            if j > 0:
                total = jnp.broadcast_to(csum[:, 0:1], (ATT_TILE, LANES))
                carry[h] = total if carry[h] is None else carry[h] + total
            t = _dot(a.astype(BF16), vh_ref[h, j * ATT_TILE:(j + 1) * ATT_TILE, :])
            state["acc"] = t if state["acc"] is None else state["acc"] + t
        if j == 0:
            o_ref[0, qi * ATT_TILE:(qi + 1) * ATT_TILE, :] = state["acc"].astype(BF16)

    for s in range(len(tiles) + 2):
        if s < len(tiles):
            score(*tiles[s])
        if 1 <= s <= len(tiles):
            in_tile_sums(*tiles[s - 1])
        if s >= 2:
            value_product(*tiles[s - 2])


def _stick_breaking(qkv, tri, batch, seq):
    qkv = qkv.reshape(batch, seq, 3 * SB_WIDTH)
    pairs = SB_WIDTH // LANES
    blk = (1, seq, LANES)
    return pl.pallas_call(
        _sb_kernel,
        grid=(batch, pairs),
        in_specs=[
            pl.BlockSpec(blk, lambda b, p: (b, 0, p)),
            pl.BlockSpec(blk, lambda b, p: (b, 0, pairs + p)),
            pl.BlockSpec(blk, lambda b, p: (b, 0, 2 * pairs + p)),
            _resident((ATT_TILE, ATT_TILE)),
        ],
        out_specs=pl.BlockSpec(blk, lambda b, p: (b, 0, p)),
        out_shape=jax.ShapeDtypeStruct((batch, seq, SB_WIDTH), BF16),
        scratch_shapes=[pltpu.VMEM((2, seq, LANES), BF16)],
        compiler_params=pltpu.CompilerParams(dimension_semantics=("arbitrary", "arbitrary"),
                                             vmem_limit_bytes=VMEM_LIMIT),
        name="stick_breaking_attention",
    )(qkv, qkv, qkv, tri)


def _diff_kernel(q_ref, k_ref, v_ref, lq1_ref, lk1_ref, lq2_ref, lk2_ref, gain_ref, o_ref):
    n_tiles = q_ref.shape[1] // ATT_TILE
    lam = (jnp.exp(jnp.sum(lq1_ref[...] * lk1_ref[...], keepdims=True))
           - jnp.exp(jnp.sum(lq2_ref[...] * lk2_ref[...], keepdims=True)) + LAMBDA_INIT)
    lane = lax.broadcasted_iota(jnp.int32, (ATT_TILE, LANES), 1)
    low = lane < HEAD_DIM
    row = lax.broadcasted_iota(jnp.int32, (ATT_TILE, ATT_TILE), 0)
    col = lax.broadcasted_iota(jnp.int32, (ATT_TILE, ATT_TILE), 1)
    causal = col <= row
    gain = gain_ref[...]

    def keys(ref, j):
        return ref[0, j * ATT_TILE:(j + 1) * ATT_TILE, :]

    def scores(qi):
        q = keys(q_ref, qi)
        zero = jnp.zeros_like(q)
        out = []
        for qh in (jnp.where(low, q, zero), jnp.where(low, zero, q)):
            z = [_dot_nt(qh, keys(k_ref, j)) for j in range(qi + 1)]
            z[qi] = jnp.where(causal, z[qi], -jnp.inf)
            out.append(z)
        return out

    def weigh(qi, zs):
        probs, denom = [], []
        for z in zs:
            m = jnp.max(functools.reduce(jnp.maximum, z), axis=1, keepdims=True)
            e = [jnp.exp(zt - m) for zt in z]
            probs.append(e)
            denom.append(jnp.sum(functools.reduce(jnp.add, e), axis=1, keepdims=True))
        r1 = 1.0 / denom[0]
        r2 = lam / denom[1]
        y = None
        for j in range(qi + 1):
            t = _dot((probs[0][j] * r1 - probs[1][j] * r2).astype(BF16), keys(v_ref, j))
            y = t if y is None else y + t
        y = y * lax.rsqrt(jnp.mean(y * y, axis=-1, keepdims=True) + RMS_EPS) * gain
        o_ref[0, qi * ATT_TILE:(qi + 1) * ATT_TILE, :] = (y * (1.0 - LAMBDA_INIT)).astype(BF16)

    pending = scores(0)
    for qi in range(n_tiles):
        following = scores(qi + 1) if qi + 1 < n_tiles else None
        weigh(qi, pending)
        pending = following


def _differential(qkv, lq1, lk1, lq2, lk2, gain, batch, seq):
    qkv = qkv.reshape(batch, seq, 3 * DIFF_WIDTH)
    blk = (1, seq, LANES)
    return pl.pallas_call(
        _diff_kernel,
        grid=(batch, DIFF_HEADS),
        in_specs=[
            pl.BlockSpec(blk, lambda b, h: (b, 0, h)),
            pl.BlockSpec(blk, lambda b, h: (b, 0, DIFF_HEADS + h)),
            pl.BlockSpec(blk, lambda b, h: (b, 0, 2 * DIFF_HEADS + h)),
            _resident((1, HEAD_DIM)), _resident((1, HEAD_DIM)), _resident((1, HEAD_DIM)), _resident((1, HEAD_DIM)),
            _resident((1, LANES)),
        ],
        out_specs=pl.BlockSpec(blk, lambda b, h: (b, 0, h)),
        out_shape=jax.ShapeDtypeStruct((batch, seq, DIFF_WIDTH), BF16),
        compiler_params=pltpu.CompilerParams(dimension_semantics=("arbitrary", "arbitrary"),
                                             vmem_limit_bytes=VMEM_LIMIT),
        name="differential_attention",
    )(qkv, qkv, qkv, lq1, lk1, lq2, lk2, gain)


def _merge_kernel(x_ref, ya_ref, yb_ref, wg_ref, wa_ref, wb_ref, wo_ref, gain_ref, bias_ref, o_ref, merged_ref):
    x = x_ref[...]
    xb = x.astype(BF16)
    ya = ya_ref[...]
    yb = yb_ref[...]
    for c in range(D_MODEL // FF_CHUNK):
        cols = slice(c * FF_CHUNK, (c + 1) * FF_CHUNK)
        gate_a = jax.nn.sigmoid(_dot(xb, wg_ref[:, cols]))
        gate_b = jax.nn.sigmoid(_dot(xb, wg_ref[:, D_MODEL + c * FF_CHUNK:D_MODEL + (c + 1) * FF_CHUNK]))
        merged = gate_a * _dot(ya, wa_ref[:, cols]) + gate_b * _dot(yb, wb_ref[:, cols])
        merged_ref[:, cols] = merged.astype(BF16)
    r = DEEPNORM_ALPHA * x + _dot(merged_ref[...], wo_ref[...])
    o_ref[...] = _layer_norm(r, gain_ref[...], bias_ref[...])


def _merge(x2d, ya, yb, w_gate, w_a, w_b, w_o, gain, bias):
    tokens = x2d.shape[0]
    rows = lambda width: pl.BlockSpec((ROW_TILE, width), lambda i: (i, 0))
    return pl.pallas_call(
        _merge_kernel,
        grid=(tokens // ROW_TILE,),
        in_specs=[
            rows(D_MODEL), rows(SB_WIDTH), rows(DIFF_WIDTH),
            _resident((D_MODEL, 2 * D_MODEL)), _resident((SB_WIDTH, D_MODEL)), _resident((DIFF_WIDTH, D_MODEL)),
            _resident((D_MODEL, D_MODEL)), _resident((1, D_MODEL)), _resident((1, D_MODEL)),
        ],
        out_specs=rows(D_MODEL),
        out_shape=jax.ShapeDtypeStruct((tokens, D_MODEL), F32),
        scratch_shapes=[pltpu.VMEM((ROW_TILE, D_MODEL), BF16)],
        compiler_params=pltpu.CompilerParams(dimension_semantics=("arbitrary",), vmem_limit_bytes=VMEM_LIMIT),
        name="gated_merge_layernorm",
    )(x2d, ya, yb, w_gate, w_a, w_b, w_o, gain, bias)


def _mlp_kernel(x_ref, w1_ref, w2_ref, gain_ref, bias_ref, o_ref):
    x = x_ref[...]
    xb = x.astype(BF16)
    acc = None
    for c in range(D_FF // FF_CHUNK):
        rows = slice(c * FF_CHUNK, (c + 1) * FF_CHUNK)
        h = jnp.maximum(_dot(xb, w1_ref[:, rows]), 0.0)
        t = _dot((h * h).astype(BF16), w2_ref[rows, :])
        acc = t if acc is None else acc + t
    r = DEEPNORM_ALPHA * x + acc
    o_ref[...] = _layer_norm(r, gain_ref[...], bias_ref[...])


def _mlp(x2d, w1, w2, gain, bias):
    tokens = x2d.shape[0]
    rows = pl.BlockSpec((ROW_TILE, D_MODEL), lambda i: (i, 0))
    return pl.pallas_call(
        _mlp_kernel,
        grid=(tokens // ROW_TILE,),
        in_specs=[rows, _resident((D_MODEL, D_FF)), _resident((D_FF, D_MODEL)),
                  _resident((1, D_MODEL)), _resident((1, D_MODEL))],
        out_specs=rows,
        out_shape=jax.ShapeDtypeStruct((tokens, D_MODEL), F32),
        compiler_params=pltpu.CompilerParams(dimension_semantics=("arbitrary",), vmem_limit_bytes=VMEM_LIMIT),
        name="mlp_layernorm",
    )(x2d, w1, w2, gain, bias)


def _rope_tables(seq):
    inv_freq = ROPE_THETA ** (-jnp.arange(0, HEAD_DIM, 2, dtype=F32) / HEAD_DIM)
    ang = jnp.arange(seq, dtype=F32)[:, None] * inv_freq[None, :]
    cos, sin = jnp.cos(ang), jnp.sin(ang)
    return jnp.tile(cos, (1, 4)), jnp.tile(jnp.concatenate([-sin, sin], axis=1), (1, 2))


def kernel(x, w_in, w_branch_a, w_branch_b, w_out, lambda_q1, lambda_k1, lambda_q2, lambda_k2, subln_gain,
           ln1_gain, ln1_bias, w_ff1, w_ff2, ln2_gain, ln2_bias):
    batch, seq, _ = x.shape
    assert seq % ROW_TILE == 0 and seq % ATT_TILE == 0 and w_in.shape[0] == DEPTH
    cos_t, sin_t = _rope_tables(seq)
    tri = jnp.asarray(np.tril(np.ones((ATT_TILE, ATT_TILE), np.float32)), BF16)
    x2d = x.reshape(batch * seq, D_MODEL)
    for l in range(DEPTH):
        w_l = w_in[l].astype(BF16)
        qkv_a, qkv_b = _project(x2d, w_l[:, :QKV_WIDTH], cos_t, sin_t, seq)
        ya = _stick_breaking(qkv_a, tri, batch, seq).reshape(batch * seq, SB_WIDTH)
        yb = _differential(qkv_b, lambda_q1[l][None], lambda_k1[l][None], lambda_q2[l][None], lambda_k2[l][None],
                           subln_gain[l][None], batch, seq).reshape(batch * seq, DIFF_WIDTH)
        x2d = _merge(x2d, ya, yb, w_l[:, QKV_WIDTH:], w_branch_a[l].astype(BF16), w_branch_b[l].astype(BF16),
                     w_out[l].astype(BF16), ln1_gain[l][None], ln1_bias[l][None])
        x2d = _mlp(x2d, w_ff1[l].astype(BF16), w_ff2[l].astype(BF16), ln2_gain[l][None], ln2_bias[l][None])
    return x2d.reshape(batch, seq, D_MODEL)
```

```python
import functools
import math

import numpy as np
import jax
import jax.numpy as jnp
from jax import lax
from jax.experimental import pallas as pl
from jax.experimental.pallas import tpu as pltpu

D_MODEL = 1024
HEAD_DIM = 64
SB_HEADS = 8
DIFF_HEADS = 4
SB_WIDTH = SB_HEADS * HEAD_DIM
DIFF_WIDTH = DIFF_HEADS * 2 * HEAD_DIM
QKV_WIDTH = 3 * SB_WIDTH + 3 * DIFF_WIDTH
D_FF = 4 * D_MODEL
ROPE_THETA = 10000.0
LN_EPS = 1e-5
RMS_EPS = 1e-5
DEPTH = 1
DEEPNORM_ALPHA = (2.0 * DEPTH) ** 0.25
LAMBDA_INIT = 0.8 - 0.6 * math.exp(-0.3 * 0)
LOG2E = math.log2(math.e)
QK_SCALE = HEAD_DIM ** -0.5 * LOG2E

LANES = 128
ROW_TILE = 1024
ATT_TILE = 256
FF_CHUNK = 512
VMEM_LIMIT = 56 * 1024 * 1024

F32 = jnp.float32
BF16 = jnp.bfloat16


def _dot(a, b):
    return jnp.dot(a, b, preferred_element_type=F32)


def _dot_nt(a, b):
    return lax.dot_general(a, b, (((1,), (1,)), ((), ())), preferred_element_type=F32)


def _resident(shape):
    return pl.BlockSpec(shape, lambda *_: (0,) * len(shape), pipeline_mode=pl.Buffered(1))


def _row_reduce(elementwise, lane_reduce, pieces):
    chunks = [p[:, c:c + LANES] for p in pieces for c in range(0, p.shape[1], LANES)]
    return lane_reduce(functools.reduce(elementwise, chunks), axis=1, keepdims=True)


def _quadrants(top_left, bottom_left, bottom_right):
    top = jnp.concatenate([top_left, jnp.zeros_like(top_left)], axis=1)
    return jnp.concatenate([top, jnp.concatenate([bottom_left, bottom_right], axis=1)], axis=0)


def _layer_norm(r, gain, bias):
    mu = jnp.mean(r, axis=-1, keepdims=True)
    c = r - mu
    var = jnp.mean(c * c, axis=-1, keepdims=True)
    return c * lax.rsqrt(var + LN_EPS) * gain + bias


def _proj_kernel(x_ref, w_ref, cos_ref, sin_ref, oa_ref, ob_ref):
    xb = x_ref[...].astype(BF16)
    lane = lax.broadcasted_iota(jnp.int32, (ROW_TILE, LANES), 1)
    first_half = (lane % HEAD_DIM) < (HEAD_DIM // 2)
    cos = cos_ref[...]
    sin = sin_ref[...]

    def rope(t):
        swapped = jnp.where(first_half, pltpu.roll(t, LANES - HEAD_DIM // 2, 1), pltpu.roll(t, HEAD_DIM // 2, 1))
        return t * cos + swapped * sin

    n_sec = SB_WIDTH // LANES
    for sec in range(6):
        acc = _dot(xb, w_ref[:, sec * SB_WIDTH:(sec + 1) * SB_WIDTH])
        out_ref = oa_ref if sec < 3 else ob_ref
        col0 = (sec % 3) * SB_WIDTH
        for g in range(n_sec):
            t = acc[:, g * LANES:(g + 1) * LANES]
            if sec in (3, 4):
                t = rope(t)
            if sec in (0, 3):
                t = t * QK_SCALE
            out_ref[:, col0 + g * LANES:col0 + (g + 1) * LANES] = t.astype(BF16)


def _project(x2d, w_qkv, cos_t, sin_t, seq):
    tokens = x2d.shape[0]
    pos_blocks = seq // ROW_TILE
    return pl.pallas_call(
        _proj_kernel,
        grid=(tokens // ROW_TILE,),
        in_specs=[
            pl.BlockSpec((ROW_TILE, D_MODEL), lambda i: (i, 0)),
            _resident((D_MODEL, QKV_WIDTH)),
            pl.BlockSpec((ROW_TILE, LANES), lambda i: (i % pos_blocks, 0)),
            pl.BlockSpec((ROW_TILE, LANES), lambda i: (i % pos_blocks, 0)),
        ],
        out_specs=[
            pl.BlockSpec((ROW_TILE, 3 * SB_WIDTH), lambda i: (i, 0)),
            pl.BlockSpec((ROW_TILE, 3 * DIFF_WIDTH), lambda i: (i, 0)),
        ],
        out_shape=[
            jax.ShapeDtypeStruct((tokens, 3 * SB_WIDTH), BF16),
            jax.ShapeDtypeStruct((tokens, 3 * DIFF_WIDTH), BF16),
        ],
        compiler_params=pltpu.CompilerParams(dimension_semantics=("arbitrary",), vmem_limit_bytes=VMEM_LIMIT),
        name="qkv_projection",
    )(x2d, w_qkv, cos_t, sin_t)


def _sb_kernel(q_ref, k_ref, v_ref, tri_ref, o_ref, vh_ref):
    n_tiles = q_ref.shape[1] // ATT_TILE
    half = ATT_TILE // 2
    tri = tri_ref[...]
    lane = lax.broadcasted_iota(jnp.int32, (ATT_TILE, LANES), 1)
    low = lane < HEAD_DIM
    row = lax.broadcasted_iota(jnp.int32, (half, half), 0)
    col = lax.broadcasted_iota(jnp.int32, (half, half), 1)
    strictly_causal = col < row

    def keys(ref, j):
        return ref[0, j * ATT_TILE:(j + 1) * ATT_TILE, :]

    for j in range(n_tiles):
        vt = keys(v_ref, j)
        zero = jnp.zeros_like(vt)
        vh_ref[0, j * ATT_TILE:(j + 1) * ATT_TILE, :] = jnp.where(low, vt, zero)
        vh_ref[1, j * ATT_TILE:(j + 1) * ATT_TILE, :] = jnp.where(low, zero, vt)

    def softplus_split(z2, mask=None):
        neg_abs = lax.bitcast_convert_type(lax.bitcast_convert_type(z2, jnp.uint32) | jnp.uint32(0x80000000), F32)
        sp = jnp.maximum(z2, 0.0) + jnp.log(1.0 + jnp.exp2(neg_abs)) * LOG2E
        if mask is not None:
            sp = jnp.where(mask, sp, 0.0)
        hi = lax.bitcast_convert_type(lax.bitcast_convert_type(sp, jnp.uint32) & jnp.uint32(0xFFFF0000), F32)
        return hi.astype(BF16), (sp - hi).astype(BF16)

    tiles = [(qi, j) for qi in range(n_tiles) for j in range(qi, -1, -1)]
    qh, scored, summed = {}, {}, {}
    state = {"acc": None, "carry": [None, None]}

    def score(qi, j):
        if j == qi:
            q = keys(q_ref, qi)
            zero = jnp.zeros_like(q)
            qh[qi] = (jnp.where(low, q, zero), jnp.where(low, zero, q))
        kt = keys(k_ref, j)
        out = []
        for h in range(2):
            z2 = _dot_nt(qh[qi][h], kt)
            if j == qi:
                parts = (softplus_split(z2[:half, :half], strictly_causal), softplus_split(z2[half:, :half]),
                         softplus_split(z2[half:, half:], strictly_causal))
                hi, lo = (_quadrants(*(p[i] for p in parts)) for i in range(2))
            else:
                hi, lo = softplus_split(z2)
            out.append((z2, hi, lo))
        scored[qi, j] = out

    def in_tile_sums(qi, j):
        summed[qi, j] = [(z2, _dot(hi, tri) + _dot(lo, tri)) for z2, hi, lo in scored.pop((qi, j))]

    def value_product(qi, j):
        if j == qi:
            state["acc"], state["carry"] = None, [None, None]
        carry = state["carry"]
        for h, (z2, csum) in enumerate(summed.pop((qi, j))):
            x = z2 - csum
            if j == qi:
                a = _quadrants(jnp.where(strictly_causal, jnp.exp2(x[:half, :half]), 0.0).astype(BF16),
                               jnp.exp2(x[half:, :half]).astype(BF16),
                               jnp.where(strictly_causal, jnp.exp2(x[half:, half:]), 0.0).astype(BF16))
            else:
                a = jnp.exp2(x - jnp.concatenate([carry[h], carry[h]], axis=1)).astype(BF16)
            if j > 0:
                total = jnp.broadcast_to(csum[:, 0:1], (ATT_TILE, LANES))
                carry[h] = total if carry[h] is None else carry[h] + total
            t = _dot(a, vh_ref[h, j * ATT_TILE:(j + 1) * ATT_TILE, :])
            state["acc"] = t if state["acc"] is None else state["acc"] + t
        if j == 0:
            o_ref[0, qi * ATT_TILE:(qi + 1) * ATT_TILE, :] = state["acc"].astype(BF16)

    for s in range(len(tiles) + 2):
        if s < len(tiles):
            score(*tiles[s])
        if 1 <= s <= len(tiles):
            in_tile_sums(*tiles[s - 1])
        if s >= 2:
            value_product(*tiles[s - 2])


def _stick_breaking(qkv, tri, batch, seq):
    qkv = qkv.reshape(batch, seq, 3 * SB_WIDTH)
    pairs = SB_WIDTH // LANES
    blk = (1, seq, LANES)
    return pl.pallas_call(
        _sb_kernel,
        grid=(batch, pairs),
        in_specs=[
            pl.BlockSpec(blk, lambda b, p: (b, 0, p)),
            pl.BlockSpec(blk, lambda b, p: (b, 0, pairs + p)),
            pl.BlockSpec(blk, lambda b, p: (b, 0, 2 * pairs + p)),
            _resident((ATT_TILE, ATT_TILE)),
        ],
        out_specs=pl.BlockSpec(blk, lambda b, p: (b, 0, p)),
        out_shape=jax.ShapeDtypeStruct((batch, seq, SB_WIDTH), BF16),
        scratch_shapes=[pltpu.VMEM((2, seq, LANES), BF16)],
        compiler_params=pltpu.CompilerParams(dimension_semantics=("arbitrary", "arbitrary"),
                                             vmem_limit_bytes=VMEM_LIMIT),
        name="stick_breaking_attention",
    )(qkv, qkv, qkv, tri)


def _diff_kernel(q_ref, k_ref, v_ref, lq1_ref, lk1_ref, lq2_ref, lk2_ref, gain_ref, o_ref):
    n_tiles = q_ref.shape[1] // ATT_TILE
    half = ATT_TILE // 2
    lam = (jnp.exp(jnp.sum(lq1_ref[...] * lk1_ref[...], keepdims=True))
           - jnp.exp(jnp.sum(lq2_ref[...] * lk2_ref[...], keepdims=True)) + LAMBDA_INIT)
    lane = lax.broadcasted_iota(jnp.int32, (ATT_TILE, LANES), 1)
    low = lane < HEAD_DIM
    row = lax.broadcasted_iota(jnp.int32, (half, half), 0)
    col = lax.broadcasted_iota(jnp.int32, (half, half), 1)
    causal = col <= row
    gain = gain_ref[...]

    def keys(ref, j):
        return ref[0, j * ATT_TILE:(j + 1) * ATT_TILE, :]

    def scores(qi):
        q = keys(q_ref, qi)
        zero = jnp.zeros_like(q)
        return [[_dot_nt(qh, keys(k_ref, j)) for j in range(qi + 1)]
                for qh in (jnp.where(low, q, zero), jnp.where(low, zero, q))]

    def weigh(qi, zs):
        weights, row_scale = [], []
        for top in (True, False):
            rows = slice(0, half) if top else slice(half, ATT_TILE)
            e, denom = [], []
            for z in zs:
                pieces = [zt[rows, :] for zt in z[:qi]]
                diag = z[qi]
                if top:
                    pieces.append(jnp.where(causal, diag[:half, :half], -jnp.inf))
                else:
                    pieces += [diag[half:, :half], jnp.where(causal, diag[half:, half:], -jnp.inf)]
                m = _row_reduce(jnp.maximum, jnp.max, pieces)
                e.append([jnp.exp2(p - m) for p in pieces])
                denom.append(_row_reduce(jnp.add, jnp.sum, e[-1]))
            rho = lam * denom[0] / denom[1]
            weights.append([(e1 - rho * e2).astype(BF16) for e1, e2 in zip(*e)])
            row_scale.append(1.0 / denom[0])
        y = None
        for j in range(qi + 1):
            if j < qi:
                w = jnp.concatenate([weights[0][j], weights[1][j]], axis=0)
            else:
                w = _quadrants(weights[0][qi], weights[1][qi], weights[1][qi + 1])
            t = _dot(w, keys(v_ref, j))
            y = t if y is None else y + t
        y = y * jnp.concatenate(row_scale, axis=0)
        y = y * lax.rsqrt(jnp.mean(y * y, axis=-1, keepdims=True) + RMS_EPS) * gain
        o_ref[0, qi * ATT_TILE:(qi + 1) * ATT_TILE, :] = (y * (1.0 - LAMBDA_INIT)).astype(BF16)

    pending = scores(0)
    for qi in range(n_tiles):
        following = scores(qi + 1) if qi + 1 < n_tiles else None
        weigh(qi, pending)
        pending = following


def _differential(qkv, lq1, lk1, lq2, lk2, gain, batch, seq):
    qkv = qkv.reshape(batch, seq, 3 * DIFF_WIDTH)
    blk = (1, seq, LANES)
    return pl.pallas_call(
        _diff_kernel,
        grid=(batch, DIFF_HEADS),
        in_specs=[
            pl.BlockSpec(blk, lambda b, h: (b, 0, h)),
            pl.BlockSpec(blk, lambda b, h: (b, 0, DIFF_HEADS + h)),
            pl.BlockSpec(blk, lambda b, h: (b, 0, 2 * DIFF_HEADS + h)),
            _resident((1, HEAD_DIM)), _resident((1, HEAD_DIM)), _resident((1, HEAD_DIM)), _resident((1, HEAD_DIM)),
            _resident((1, LANES)),
        ],
        out_specs=pl.BlockSpec(blk, lambda b, h: (b, 0, h)),
        out_shape=jax.ShapeDtypeStruct((batch, seq, DIFF_WIDTH), BF16),
        compiler_params=pltpu.CompilerParams(dimension_semantics=("arbitrary", "arbitrary"),
                                             vmem_limit_bytes=VMEM_LIMIT),
        name="differential_attention",
    )(qkv, qkv, qkv, lq1, lk1, lq2, lk2, gain)


def _merge_kernel(x_ref, ya_ref, yb_ref, wg_ref, wa_ref, wb_ref, wo_ref, gain_ref, bias_ref, o_ref, merged_ref):
    x = x_ref[...]
    xb = x.astype(BF16)
    ya = ya_ref[...]
    yb = yb_ref[...]
    for c in range(D_MODEL // FF_CHUNK):
        cols = slice(c * FF_CHUNK, (c + 1) * FF_CHUNK)
        gate_a = jax.nn.sigmoid(_dot(xb, wg_ref[:, cols]))
        gate_b = jax.nn.sigmoid(_dot(xb, wg_ref[:, D_MODEL + c * FF_CHUNK:D_MODEL + (c + 1) * FF_CHUNK]))
        merged = gate_a * _dot(ya, wa_ref[:, cols]) + gate_b * _dot(yb, wb_ref[:, cols])
        merged_ref[:, cols] = merged.astype(BF16)
    r = DEEPNORM_ALPHA * x + _dot(merged_ref[...], wo_ref[...])
    o_ref[...] = _layer_norm(r, gain_ref[...], bias_ref[...])


def _merge(x2d, ya, yb, w_gate, w_a, w_b, w_o, gain, bias):
    tokens = x2d.shape[0]
    rows = lambda width: pl.BlockSpec((ROW_TILE, width), lambda i: (i, 0))
    return pl.pallas_call(
        _merge_kernel,
        grid=(tokens // ROW_TILE,),
        in_specs=[
            rows(D_MODEL), rows(SB_WIDTH), rows(DIFF_WIDTH),
            _resident((D_MODEL, 2 * D_MODEL)), _resident((SB_WIDTH, D_MODEL)), _resident((DIFF_WIDTH, D_MODEL)),
            _resident((D_MODEL, D_MODEL)), _resident((1, D_MODEL)), _resident((1, D_MODEL)),
        ],
        out_specs=rows(D_MODEL),
        out_shape=jax.ShapeDtypeStruct((tokens, D_MODEL), F32),
        scratch_shapes=[pltpu.VMEM((ROW_TILE, D_MODEL), BF16)],
        compiler_params=pltpu.CompilerParams(dimension_semantics=("arbitrary",), vmem_limit_bytes=VMEM_LIMIT),
        name="gated_merge_layernorm",
    )(x2d, ya, yb, w_gate, w_a, w_b, w_o, gain, bias)


def _mlp_kernel(x_ref, w1_ref, w2_ref, gain_ref, bias_ref, o_ref):
    x = x_ref[...]
    xb = x.astype(BF16)
    acc = None
    for c in range(D_FF // FF_CHUNK):
        rows = slice(c * FF_CHUNK, (c + 1) * FF_CHUNK)
        h = jnp.maximum(_dot(xb, w1_ref[:, rows]), 0.0)
        t = _dot((h * h).astype(BF16), w2_ref[rows, :])
        acc = t if acc is None else acc + t
    r = DEEPNORM_ALPHA * x + acc
    o_ref[...] = _layer_norm(r, gain_ref[...], bias_ref[...])


def _mlp(x2d, w1, w2, gain, bias):
    tokens = x2d.shape[0]
    rows = pl.BlockSpec((ROW_TILE, D_MODEL), lambda i: (i, 0))
    return pl.pallas_call(
        _mlp_kernel,
        grid=(tokens // ROW_TILE,),
        in_specs=[rows, _resident((D_MODEL, D_FF)), _resident((D_FF, D_MODEL)),
                  _resident((1, D_MODEL)), _resident((1, D_MODEL))],
        out_specs=rows,
        out_shape=jax.ShapeDtypeStruct((tokens, D_MODEL), F32),
        compiler_params=pltpu.CompilerParams(dimension_semantics=("arbitrary",), vmem_limit_bytes=VMEM_LIMIT),
        name="mlp_layernorm",
    )(x2d, w1, w2, gain, bias)


def _rope_tables(seq):
    inv_freq = ROPE_THETA ** (-jnp.arange(0, HEAD_DIM, 2, dtype=F32) / HEAD_DIM)
    ang = jnp.arange(seq, dtype=F32)[:, None] * inv_freq[None, :]
    cos, sin = jnp.cos(ang), jnp.sin(ang)
    return jnp.tile(cos, (1, 4)), jnp.tile(jnp.concatenate([-sin, sin], axis=1), (1, 2))


def kernel(x, w_in, w_branch_a, w_branch_b, w_out, lambda_q1, lambda_k1, lambda_q2, lambda_k2, subln_gain,
           ln1_gain, ln1_bias, w_ff1, w_ff2, ln2_gain, ln2_bias):
    batch, seq, _ = x.shape
    assert seq % ROW_TILE == 0 and seq % ATT_TILE == 0 and w_in.shape[0] == DEPTH
    cos_t, sin_t = _rope_tables(seq)
    tri = jnp.asarray(np.tril(np.ones((ATT_TILE, ATT_TILE), np.float32)), BF16)
    x2d = x.reshape(batch * seq, D_MODEL)
    for l in range(DEPTH):
        w_l = w_in[l].astype(BF16)
        qkv_a, qkv_b = _project(x2d, w_l[:, :QKV_WIDTH], cos_t, sin_t, seq)
        ya = _stick_breaking(qkv_a, tri, batch, seq).reshape(batch * seq, SB_WIDTH)
        yb = _differential(qkv_b, lambda_q1[l][None], lambda_k1[l][None], lambda_q2[l][None], lambda_k2[l][None],
                           subln_gain[l][None], batch, seq).reshape(batch * seq, DIFF_WIDTH)
        x2d = _merge(x2d, ya, yb, w_l[:, QKV_WIDTH:], w_branch_a[l].astype(BF16), w_branch_b[l].astype(BF16),
                     w_out[l].astype(BF16), ln1_gain[l][None], ln1_bias[l][None])
        x2d = _mlp(x2d, w_ff1[l].astype(BF16), w_ff2[l].astype(BF16), ln2_gain[l][None], ln2_bias[l][None])
    return x2d.reshape(batch, seq, D_MODEL)
```

```python
import functools
import math

import numpy as np
import jax
import jax.numpy as jnp
from jax import lax
from jax.experimental import pallas as pl
from jax.experimental.pallas import tpu as pltpu

D_MODEL = 1024
HEAD_DIM = 64
SB_HEADS = 8
DIFF_HEADS = 4
SB_WIDTH = SB_HEADS * HEAD_DIM
DIFF_WIDTH = DIFF_HEADS * 2 * HEAD_DIM
QKV_WIDTH = 3 * SB_WIDTH + 3 * DIFF_WIDTH
D_FF = 4 * D_MODEL
ROPE_THETA = 10000.0
LN_EPS = 1e-5
RMS_EPS = 1e-5
DEPTH = 1
DEEPNORM_ALPHA = (2.0 * DEPTH) ** 0.25
LAMBDA_INIT = 0.8 - 0.6 * math.exp(-0.3 * 0)
LOG2E = math.log2(math.e)
QK_SCALE = HEAD_DIM ** -0.5 * LOG2E

LANES = 128
BF16_SUBLANES = 16
ROW_TILE = 1024
ATT_TILE = 256
DIFF_HEADS_PER_STEP = 2
FF_CHUNK = 512
VMEM_LIMIT = 56 * 1024 * 1024

F32 = jnp.float32
BF16 = jnp.bfloat16


def _dot(a, b):
    return jnp.dot(a, b, preferred_element_type=F32)


def _dot_nt(a, b):
    return lax.dot_general(a, b, (((1,), (1,)), ((), ())), preferred_element_type=F32)


def _resident(shape):
    return pl.BlockSpec(shape, lambda *_: (0,) * len(shape), pipeline_mode=pl.Buffered(1))


def _row_reduce(elementwise, lane_reduce, pieces):
    chunks = [p[:, c:c + LANES] for p in pieces for c in range(0, p.shape[1], LANES)]
    return lane_reduce(functools.reduce(elementwise, chunks), axis=1, keepdims=True)


def _quadrants(top_left, bottom_left, bottom_right):
    top = jnp.concatenate([top_left, jnp.zeros_like(top_left)], axis=1)
    return jnp.concatenate([top, jnp.concatenate([bottom_left, bottom_right], axis=1)], axis=0)


def _layer_norm(r, gain, bias):
    mu = jnp.mean(r, axis=-1, keepdims=True)
    c = r - mu
    var = jnp.mean(c * c, axis=-1, keepdims=True)
    return c * lax.rsqrt(var + LN_EPS) * gain + bias


def _proj_kernel(x_ref, w_ref, cos_ref, sin_ref, *refs):
    n_cast = (len(refs) - 2) // 2
    cast_in, (oa_ref, ob_ref), cast_out = refs[:n_cast], refs[n_cast:n_cast + 2], refs[n_cast + 2:]
    for src, dst in zip(cast_in, cast_out):
        dst[...] = src[...].astype(BF16)

    xb = x_ref[...].astype(BF16)
    lane = lax.broadcasted_iota(jnp.int32, (ROW_TILE, LANES), 1)
    first_half = (lane % HEAD_DIM) < (HEAD_DIM // 2)
    cos = cos_ref[...]
    sin = sin_ref[...]

    def rope(t):
        swapped = jnp.where(first_half, pltpu.roll(t, LANES - HEAD_DIM // 2, 1), pltpu.roll(t, HEAD_DIM // 2, 1))
        return t * cos + swapped * sin

    n_sec = SB_WIDTH // LANES
    for sec in range(6):
        acc = _dot(xb, w_ref[:, sec * SB_WIDTH:(sec + 1) * SB_WIDTH].astype(BF16))
        out_ref = oa_ref if sec < 3 else ob_ref
        col0 = (sec % 3) * SB_WIDTH
        for g in range(n_sec):
            t = acc[:, g * LANES:(g + 1) * LANES]
            if sec in (3, 4):
                t = rope(t)
            if sec in (0, 3):
                t = t * QK_SCALE
            out_ref[:, col0 + g * LANES:col0 + (g + 1) * LANES] = t.astype(BF16)


def _project(x2d, w_in, later_weights, cos_t, sin_t, seq):
    tokens = x2d.shape[0]
    steps = tokens // ROW_TILE
    pos_blocks = seq // ROW_TILE
    gate_blocks = [QKV_WIDTH // D_MODEL + c for c in range((w_in.shape[1] - QKV_WIDTH) // D_MODEL)]
    cast_arrays = [w_in] * len(gate_blocks) + list(later_weights)
    cast_in, cast_out, cast_shapes = [], [], []
    for n, w in enumerate(cast_arrays):
        rows = w.shape[0] // steps
        assert rows * steps == w.shape[0] and rows % BF16_SUBLANES == 0
        width = D_MODEL if n < len(gate_blocks) else w.shape[1]
        col = gate_blocks[n] if n < len(gate_blocks) else 0
        cast_in.append(pl.BlockSpec((rows, width), lambda i, col=col: (i, col)))
        cast_out.append(pl.BlockSpec((rows, width), lambda i: (i, 0)))
        cast_shapes.append(jax.ShapeDtypeStruct((w.shape[0], width), BF16))
    outs = pl.pallas_call(
        _proj_kernel,
        grid=(steps,),
        in_specs=[
            pl.BlockSpec((ROW_TILE, D_MODEL), lambda i: (i, 0)),
            pl.BlockSpec((D_MODEL, QKV_WIDTH), lambda i: (0, 0), pipeline_mode=pl.Buffered(1)),
            pl.BlockSpec((ROW_TILE, LANES), lambda i: (i % pos_blocks, 0)),
            pl.BlockSpec((ROW_TILE, LANES), lambda i: (i % pos_blocks, 0)),
        ] + cast_in,
        out_specs=[
            pl.BlockSpec((ROW_TILE, 3 * SB_WIDTH), lambda i: (i, 0)),
            pl.BlockSpec((ROW_TILE, 3 * DIFF_WIDTH), lambda i: (i, 0)),
        ] + cast_out,
        out_shape=[
            jax.ShapeDtypeStruct((tokens, 3 * SB_WIDTH), BF16),
            jax.ShapeDtypeStruct((tokens, 3 * DIFF_WIDTH), BF16),
        ] + cast_shapes,
        compiler_params=pltpu.CompilerParams(dimension_semantics=("arbitrary",), vmem_limit_bytes=VMEM_LIMIT),
        name="qkv_projection",
    )(x2d, w_in, cos_t, sin_t, *cast_arrays)
    return outs[0], outs[1], outs[2:2 + len(gate_blocks)], outs[2 + len(gate_blocks):]


def _sb_kernel(q_ref, k_ref, v_ref, tri_ref, o_ref, vh_ref):
    n_tiles = q_ref.shape[1] // ATT_TILE
    half = ATT_TILE // 2
    tri = tri_ref[...]
    lane = lax.broadcasted_iota(jnp.int32, (ATT_TILE, LANES), 1)
    low = lane < HEAD_DIM
    row = lax.broadcasted_iota(jnp.int32, (half, half), 0)
    col = lax.broadcasted_iota(jnp.int32, (half, half), 1)
    strictly_causal = col < row

    def keys(ref, j):
        return ref[0, j * ATT_TILE:(j + 1) * ATT_TILE, :]

    for j in range(n_tiles):
        vt = keys(v_ref, j)
        zero = jnp.zeros_like(vt)
        vh_ref[0, j * ATT_TILE:(j + 1) * ATT_TILE, :] = jnp.where(low, vt, zero)
        vh_ref[1, j * ATT_TILE:(j + 1) * ATT_TILE, :] = jnp.where(low, zero, vt)

    def softplus_split(z2, mask=None):
        sp = jnp.maximum(z2, 0.0) + jnp.log(1.0 + jnp.exp2(-jnp.abs(z2))) * LOG2E
        if mask is not None:
            sp = jnp.where(mask, sp, 0.0)
        hi = lax.bitcast_convert_type(lax.bitcast_convert_type(sp, jnp.uint32) & jnp.uint32(0xFFFF0000), F32)
        return hi.astype(BF16), (sp - hi).astype(BF16)

    tiles = [(qi, j) for qi in range(n_tiles) for j in range(qi, -1, -1)]
    qh, scored, summed = {}, {}, {}
    state = {"acc": None, "carry": [None, None]}

    def score(qi, j):
        if j == qi:
            q = keys(q_ref, qi)
            zero = jnp.zeros_like(q)
            qh[qi] = (jnp.where(low, q, zero), jnp.where(low, zero, q))
        kt = keys(k_ref, j)
        out = []
        for h in range(2):
            z2 = _dot_nt(qh[qi][h], kt)
            if j == qi:
                parts = (softplus_split(z2[:half, :half], strictly_causal), softplus_split(z2[half:, :half]),
                         softplus_split(z2[half:, half:], strictly_causal))
                hi, lo = (_quadrants(*(p[i] for p in parts)) for i in range(2))
            else:
                hi, lo = softplus_split(z2)
            out.append((z2, hi, lo))
        scored[qi, j] = out

    def in_tile_sums(qi, j):
        summed[qi, j] = [(z2, _dot(hi, tri) + _dot(lo, tri)) for z2, hi, lo in scored.pop((qi, j))]

    def value_product(qi, j):
        if j == qi:
            state["acc"], state["carry"] = None, [None, None]
        carry = state["carry"]
        for h, (z2, csum) in enumerate(summed.pop((qi, j))):
            x = z2 - csum
            if j == qi:
                a = _quadrants(jnp.where(strictly_causal, jnp.exp2(x[:half, :half]), 0.0).astype(BF16),
                               jnp.exp2(x[half:, :half]).astype(BF16),
                               jnp.where(strictly_causal, jnp.exp2(x[half:, half:]), 0.0).astype(BF16))
            else:
                a = jnp.exp2(x - jnp.concatenate([carry[h], carry[h]], axis=1)).astype(BF16)
            if j > 0:
                total = jnp.broadcast_to(csum[:, 0:1], (ATT_TILE, LANES))
                carry[h] = total if carry[h] is None else carry[h] + total
            t = _dot(a, vh_ref[h, j * ATT_TILE:(j + 1) * ATT_TILE, :])
            state["acc"] = t if state["acc"] is None else state["acc"] + t
        if j == 0:
            o_ref[0, qi * ATT_TILE:(qi + 1) * ATT_TILE, :] = state["acc"].astype(BF16)

    for s in range(len(tiles) + 2):
        if s < len(tiles):
            score(*tiles[s])
        if 1 <= s <= len(tiles):
            in_tile_sums(*tiles[s - 1])
        if s >= 2:
            value_product(*tiles[s - 2])


def _stick_breaking(qkv, tri, batch, seq):
    qkv = qkv.reshape(batch, seq, 3 * SB_WIDTH)
    pairs = SB_WIDTH // LANES
    blk = (1, seq, LANES)
    return pl.pallas_call(
        _sb_kernel,
        grid=(batch, pairs),
        in_specs=[
            pl.BlockSpec(blk, lambda b, p: (b, 0, p)),
            pl.BlockSpec(blk, lambda b, p: (b, 0, pairs + p)),
            pl.BlockSpec(blk, lambda b, p: (b, 0, 2 * pairs + p)),
            _resident((ATT_TILE, ATT_TILE)),
        ],
        out_specs=pl.BlockSpec(blk, lambda b, p: (b, 0, p)),
        out_shape=jax.ShapeDtypeStruct((batch, seq, SB_WIDTH), BF16),
        scratch_shapes=[pltpu.VMEM((2, seq, LANES), BF16)],
        compiler_params=pltpu.CompilerParams(dimension_semantics=("arbitrary", "arbitrary"),
                                             vmem_limit_bytes=VMEM_LIMIT),
        name="stick_breaking_attention",
    )(qkv, qkv, qkv, tri)


def _diff_kernel(q_ref, k_ref, v_ref, lq1_ref, lk1_ref, lq2_ref, lk2_ref, gain_ref, o_ref):
    n_tiles = q_ref.shape[1] // ATT_TILE
    half = ATT_TILE // 2
    lam = (jnp.exp(jnp.sum(lq1_ref[...] * lk1_ref[...], keepdims=True))
           - jnp.exp(jnp.sum(lq2_ref[...] * lk2_ref[...], keepdims=True)) + LAMBDA_INIT)
    lane = lax.broadcasted_iota(jnp.int32, (ATT_TILE, LANES), 1)
    low = lane < HEAD_DIM
    row = lax.broadcasted_iota(jnp.int32, (half, half), 0)
    col = lax.broadcasted_iota(jnp.int32, (half, half), 1)
    causal = col <= row
    gain = gain_ref[...]

    def keys(ref, head, j):
        return ref[0, j * ATT_TILE:(j + 1) * ATT_TILE, head * LANES:(head + 1) * LANES]

    def scores(head, qi):
        q = keys(q_ref, head, qi)
        zero = jnp.zeros_like(q)
        return [[_dot_nt(qh, keys(k_ref, head, j)) for j in range(qi + 1)]
                for qh in (jnp.where(low, q, zero), jnp.where(low, zero, q))]

    def weigh(head, qi, zs):
        weights, row_scale = [], []
        for top in (True, False):
            rows = slice(0, half) if top else slice(half, ATT_TILE)
            e, denom = [], []
            for z in zs:
                pieces = [zt[rows, :] for zt in z[:qi]]
                diag = z[qi]
                if top:
                    pieces.append(jnp.where(causal, diag[:half, :half], -jnp.inf))
                else:
                    pieces += [diag[half:, :half], jnp.where(causal, diag[half:, half:], -jnp.inf)]
                m = _row_reduce(jnp.maximum, jnp.max, pieces)
                e.append([jnp.exp2(p - m) for p in pieces])
                denom.append(_row_reduce(jnp.add, jnp.sum, e[-1]))
            rho = lam * denom[0] / denom[1]
            weights.append([(e1 - rho * e2).astype(BF16) for e1, e2 in zip(*e)])
            row_scale.append(1.0 / denom[0])
        y = None
        for j in range(qi + 1):
            if j < qi:
                w = jnp.concatenate([weights[0][j], weights[1][j]], axis=0)
            else:
                w = _quadrants(weights[0][qi], weights[1][qi], weights[1][qi + 1])
            t = _dot(w, keys(v_ref, head, j))
            y = t if y is None else y + t
        y = y * jnp.concatenate(row_scale, axis=0)
        y = y * lax.rsqrt(jnp.mean(y * y, axis=-1, keepdims=True) + RMS_EPS) * gain
        o_ref[0, qi * ATT_TILE:(qi + 1) * ATT_TILE, head * LANES:(head + 1) * LANES] = (
            y * (1.0 - LAMBDA_INIT)).astype(BF16)

    heads = range(q_ref.shape[2] // LANES)
    pending = [scores(head, 0) for head in heads]
    for qi in range(n_tiles):
        following = [scores(head, qi + 1) if qi + 1 < n_tiles else None for head in heads]
        for head in heads:
            weigh(head, qi, pending[head])
        pending = following


def _differential(qkv, lq1, lk1, lq2, lk2, gain, batch, seq):
    qkv = qkv.reshape(batch, seq, 3 * DIFF_WIDTH)
    blk = (1, seq, DIFF_HEADS_PER_STEP * LANES)
    groups = DIFF_HEADS // DIFF_HEADS_PER_STEP
    return pl.pallas_call(
        _diff_kernel,
        grid=(batch, groups),
        in_specs=[
            pl.BlockSpec(blk, lambda b, h: (b, 0, h)),
            pl.BlockSpec(blk, lambda b, h: (b, 0, groups + h)),
            pl.BlockSpec(blk, lambda b, h: (b, 0, 2 * groups + h)),
            _resident((1, HEAD_DIM)), _resident((1, HEAD_DIM)), _resident((1, HEAD_DIM)), _resident((1, HEAD_DIM)),
            _resident((1, LANES)),
        ],
        out_specs=pl.BlockSpec(blk, lambda b, h: (b, 0, h)),
        out_shape=jax.ShapeDtypeStruct((batch, seq, DIFF_WIDTH), BF16),
        compiler_params=pltpu.CompilerParams(dimension_semantics=("arbitrary", "arbitrary"),
                                             vmem_limit_bytes=VMEM_LIMIT),
        name="differential_attention",
    )(qkv, qkv, qkv, lq1, lk1, lq2, lk2, gain)


def _merge_kernel(x_ref, ya_ref, yb_ref, wga_ref, wgb_ref, wa_ref, wb_ref, wo_ref, gain_ref, bias_ref, o_ref,
                  merged_ref):
    x = x_ref[...]
    xb = x.astype(BF16)
    ya = ya_ref[...]
    yb = yb_ref[...]
    for c in range(D_MODEL // FF_CHUNK):
        cols = slice(c * FF_CHUNK, (c + 1) * FF_CHUNK)
        gate_a = jax.nn.sigmoid(_dot(xb, wga_ref[:, cols]))
        gate_b = jax.nn.sigmoid(_dot(xb, wgb_ref[:, cols]))
        merged = gate_a * _dot(ya, wa_ref[:, cols]) + gate_b * _dot(yb, wb_ref[:, cols])
        merged_ref[:, cols] = merged.astype(BF16)
    r = DEEPNORM_ALPHA * x + _dot(merged_ref[...], wo_ref[...])
    o_ref[...] = _layer_norm(r, gain_ref[...], bias_ref[...])


def _merge(x2d, ya, yb, w_gate_a, w_gate_b, w_a, w_b, w_o, gain, bias):
    tokens = x2d.shape[0]
    rows = lambda width: pl.BlockSpec((ROW_TILE, width), lambda i: (i, 0))
    return pl.pallas_call(
        _merge_kernel,
        grid=(tokens // ROW_TILE,),
        in_specs=[
            rows(D_MODEL), rows(SB_WIDTH), rows(DIFF_WIDTH),
            _resident((D_MODEL, D_MODEL)), _resident((D_MODEL, D_MODEL)),
            _resident((SB_WIDTH, D_MODEL)), _resident((DIFF_WIDTH, D_MODEL)),
            _resident((D_MODEL, D_MODEL)), _resident((1, D_MODEL)), _resident((1, D_MODEL)),
        ],
        out_specs=rows(D_MODEL),
        out_shape=jax.ShapeDtypeStruct((tokens, D_MODEL), F32),
        scratch_shapes=[pltpu.VMEM((ROW_TILE, D_MODEL), BF16)],
        compiler_params=pltpu.CompilerParams(dimension_semantics=("arbitrary",), vmem_limit_bytes=VMEM_LIMIT),
        name="gated_merge_layernorm",
    )(x2d, ya, yb, w_gate_a, w_gate_b, w_a, w_b, w_o, gain, bias)


def _mlp_kernel(x_ref, w1_ref, w2_ref, gain_ref, bias_ref, o_ref):
    x = x_ref[...]
    xb = x.astype(BF16)
    acc = None
    for c in range(D_FF // FF_CHUNK):
        rows = slice(c * FF_CHUNK, (c + 1) * FF_CHUNK)
        h = jnp.maximum(_dot(xb, w1_ref[:, rows]), 0.0)
        t = _dot((h * h).astype(BF16), w2_ref[rows, :])
        acc = t if acc is None else acc + t
    r = DEEPNORM_ALPHA * x + acc
    o_ref[...] = _layer_norm(r, gain_ref[...], bias_ref[...])


def _mlp(x2d, w1, w2, gain, bias):
    tokens = x2d.shape[0]
    rows = pl.BlockSpec((ROW_TILE, D_MODEL), lambda i: (i, 0))
    return pl.pallas_call(
        _mlp_kernel,
        grid=(tokens // ROW_TILE,),
        in_specs=[rows, _resident((D_MODEL, D_FF)), _resident((D_FF, D_MODEL)),
                  _resident((1, D_MODEL)), _resident((1, D_MODEL))],
        out_specs=rows,
        out_shape=jax.ShapeDtypeStruct((tokens, D_MODEL), F32),
        compiler_params=pltpu.CompilerParams(dimension_semantics=("arbitrary",), vmem_limit_bytes=VMEM_LIMIT),
        name="mlp_layernorm",
    )(x2d, w1, w2, gain, bias)


def _rope_tables(seq):
    inv_freq = ROPE_THETA ** (-jnp.arange(0, HEAD_DIM, 2, dtype=F32) / HEAD_DIM)
    ang = jnp.arange(seq, dtype=F32)[:, None] * inv_freq[None, :]
    cos, sin = jnp.cos(ang), jnp.sin(ang)
    return jnp.tile(cos, (1, 4)), jnp.tile(jnp.concatenate([-sin, sin], axis=1), (1, 2))


def kernel(x, w_in, w_branch_a, w_branch_b, w_out, lambda_q1, lambda_k1, lambda_q2, lambda_k2, subln_gain,
           ln1_gain, ln1_bias, w_ff1, w_ff2, ln2_gain, ln2_bias):
    batch, seq, _ = x.shape
    assert seq % ROW_TILE == 0 and seq % ATT_TILE == 0 and w_in.shape[0] == DEPTH
    cos_t, sin_t = _rope_tables(seq)
    tri = jnp.asarray(np.tril(np.ones((ATT_TILE, ATT_TILE), np.float32)), BF16)
    x2d = x.reshape(batch * seq, D_MODEL)
    for l in range(DEPTH):
        later = (w_branch_a[l], w_branch_b[l], w_out[l], w_ff1[l], w_ff2[l])
        qkv_a, qkv_b, w_gates, (w_a, w_b, w_o, w_1, w_2) = _project(x2d, w_in[l], later, cos_t, sin_t, seq)
        ya = _stick_breaking(qkv_a, tri, batch, seq).reshape(batch * seq, SB_WIDTH)
        yb = _differential(qkv_b, lambda_q1[l][None], lambda_k1[l][None], lambda_q2[l][None], lambda_k2[l][None],
                           subln_gain[l][None], batch, seq).reshape(batch * seq, DIFF_WIDTH)
        x2d = _merge(x2d, ya, yb, *w_gates, w_a, w_b, w_o, ln1_gain[l][None], ln1_bias[l][None])
        x2d = _mlp(x2d, w_1, w_2, ln2_gain[l][None], ln2_bias[l][None])
    return x2d.reshape(batch, seq, D_MODEL)
```

```python
import functools
import math

import numpy as np
import jax
import jax.numpy as jnp
from jax import lax
from jax.experimental import pallas as pl
from jax.experimental.pallas import tpu as pltpu

D_MODEL = 1024
HEAD_DIM = 64
SB_HEADS = 8
DIFF_HEADS = 4
SB_WIDTH = SB_HEADS * HEAD_DIM
DIFF_WIDTH = DIFF_HEADS * 2 * HEAD_DIM
QKV_WIDTH = 3 * SB_WIDTH + 3 * DIFF_WIDTH
D_FF = 4 * D_MODEL
ROPE_THETA = 10000.0
LN_EPS = 1e-5
RMS_EPS = 1e-5
DEPTH = 1
DEEPNORM_ALPHA = (2.0 * DEPTH) ** 0.25
LAMBDA_INIT = 0.8 - 0.6 * math.exp(-0.3 * 0)
LOG2E = math.log2(math.e)
SOFTPLUS_CLAMP = 64.0
QK_SCALE = HEAD_DIM ** -0.5 * LOG2E

LANES = 128
BF16_SUBLANES = 16
ROW_TILE = 1024
SUB_ROWS = 512
ATT_TILE = 256
DIFF_HEADS_PER_STEP = 2
FF_CHUNK = 512
VMEM_LIMIT = 56 * 1024 * 1024

F32 = jnp.float32
BF16 = jnp.bfloat16


def _dot(a, b):
    return jnp.dot(a, b, preferred_element_type=F32)


def _dot_nt(a, b):
    return lax.dot_general(a, b, (((1,), (1,)), ((), ())), preferred_element_type=F32)


def _resident(shape):
    return pl.BlockSpec(shape, lambda *_: (0,) * len(shape), pipeline_mode=pl.Buffered(1))


def _row_reduce(elementwise, lane_reduce, pieces):
    chunks = [p[:, c:c + LANES] for p in pieces for c in range(0, p.shape[1], LANES)]
    return lane_reduce(functools.reduce(elementwise, chunks), axis=1, keepdims=True)


def _quadrants(top_left, bottom_left, bottom_right):
    top = jnp.concatenate([top_left, jnp.zeros_like(top_left)], axis=1)
    return jnp.concatenate([top, jnp.concatenate([bottom_left, bottom_right], axis=1)], axis=0)


def _layer_norm(r, gain, bias):
    mu = jnp.mean(r, axis=-1, keepdims=True)
    c = r - mu
    var = jnp.mean(c * c, axis=-1, keepdims=True)
    return c * lax.rsqrt(var + LN_EPS) * gain + bias


def _proj_kernel(x_ref, w_ref, cos_ref, sin_ref, *refs):
    n_cast = (len(refs) - 2) // 2
    cast_in, (oa_ref, ob_ref), cast_out = refs[:n_cast], refs[n_cast:n_cast + 2], refs[n_cast + 2:]
    for src, dst in zip(cast_in, cast_out):
        dst[...] = src[...].astype(BF16)

    xb = x_ref[...].astype(BF16)
    lane = lax.broadcasted_iota(jnp.int32, (ROW_TILE, LANES), 1)
    first_half = (lane % HEAD_DIM) < (HEAD_DIM // 2)
    cos = cos_ref[...]
    sin = sin_ref[...]

    def rope(t):
        swapped = jnp.where(first_half, pltpu.roll(t, LANES - HEAD_DIM // 2, 1), pltpu.roll(t, HEAD_DIM // 2, 1))
        return t * cos + swapped * sin

    n_sec = SB_WIDTH // LANES
    for sec in range(6):
        acc = _dot(xb, w_ref[:, sec * SB_WIDTH:(sec + 1) * SB_WIDTH].astype(BF16))
        out_ref = oa_ref if sec < 3 else ob_ref
        col0 = (sec % 3) * SB_WIDTH
        for g in range(n_sec):
            t = acc[:, g * LANES:(g + 1) * LANES]
            if sec in (3, 4):
                t = rope(t)
            if sec in (0, 3):
                t = t * QK_SCALE
            out_ref[:, col0 + g * LANES:col0 + (g + 1) * LANES] = t.astype(BF16)


def _project(x2d, w_in, later_weights, cos_t, sin_t, seq):
    tokens = x2d.shape[0]
    steps = tokens // ROW_TILE
    pos_blocks = seq // ROW_TILE
    gate_blocks = [QKV_WIDTH // D_MODEL + c for c in range((w_in.shape[1] - QKV_WIDTH) // D_MODEL)]
    cast_arrays = [w_in] * len(gate_blocks) + list(later_weights)
    cast_in, cast_out, cast_shapes = [], [], []
    for n, w in enumerate(cast_arrays):
        rows = w.shape[0] // steps
        assert rows * steps == w.shape[0] and rows % BF16_SUBLANES == 0
        width = D_MODEL if n < len(gate_blocks) else w.shape[1]
        col = gate_blocks[n] if n < len(gate_blocks) else 0
        cast_in.append(pl.BlockSpec((rows, width), lambda i, col=col: (i, col)))
        cast_out.append(pl.BlockSpec((rows, width), lambda i: (i, 0)))
        cast_shapes.append(jax.ShapeDtypeStruct((w.shape[0], width), BF16))
    outs = pl.pallas_call(
        _proj_kernel,
        grid=(steps,),
        in_specs=[
            pl.BlockSpec((ROW_TILE, D_MODEL), lambda i: (i, 0)),
            pl.BlockSpec((D_MODEL, QKV_WIDTH), lambda i: (0, 0), pipeline_mode=pl.Buffered(1)),
            pl.BlockSpec((ROW_TILE, LANES), lambda i: (i % pos_blocks, 0)),
            pl.BlockSpec((ROW_TILE, LANES), lambda i: (i % pos_blocks, 0)),
        ] + cast_in,
        out_specs=[
            pl.BlockSpec((ROW_TILE, 3 * SB_WIDTH), lambda i: (i, 0)),
            pl.BlockSpec((ROW_TILE, 3 * DIFF_WIDTH), lambda i: (i, 0)),
        ] + cast_out,
        out_shape=[
            jax.ShapeDtypeStruct((tokens, 3 * SB_WIDTH), BF16),
            jax.ShapeDtypeStruct((tokens, 3 * DIFF_WIDTH), BF16),
        ] + cast_shapes,
        compiler_params=pltpu.CompilerParams(dimension_semantics=("arbitrary",), vmem_limit_bytes=VMEM_LIMIT),
        name="qkv_projection",
    )(x2d, w_in, cos_t, sin_t, *cast_arrays)
    return outs[0], outs[1], outs[2:2 + len(gate_blocks)], outs[2 + len(gate_blocks):]


def _sb_kernel(q_ref, k_ref, v_ref, tri_ref, o_ref, vh_ref):
    n_tiles = q_ref.shape[1] // ATT_TILE
    half = ATT_TILE // 2
    tri = tri_ref[...]
    lane = lax.broadcasted_iota(jnp.int32, (ATT_TILE, LANES), 1)
    low = lane < HEAD_DIM
    row = lax.broadcasted_iota(jnp.int32, (half, half), 0)
    col = lax.broadcasted_iota(jnp.int32, (half, half), 1)
    strictly_causal = col < row

    def keys(ref, j):
        return ref[0, j * ATT_TILE:(j + 1) * ATT_TILE, :]

    for j in range(n_tiles):
        vt = keys(v_ref, j)
        zero = jnp.zeros_like(vt)
        vh_ref[0, j * ATT_TILE:(j + 1) * ATT_TILE, :] = jnp.where(low, vt, zero)
        vh_ref[1, j * ATT_TILE:(j + 1) * ATT_TILE, :] = jnp.where(low, zero, vt)

    def softplus_split(z2, mask=None):
        sp = jnp.maximum(z2, jnp.log(1.0 + jnp.exp2(jnp.minimum(z2, SOFTPLUS_CLAMP))) * LOG2E)
        if mask is not None:
            sp = jnp.where(mask, sp, 0.0)
        hi = lax.bitcast_convert_type(lax.bitcast_convert_type(sp, jnp.uint32) & jnp.uint32(0xFFFF0000), F32)
        return hi.astype(BF16), (sp - hi).astype(BF16)

    tiles = [(qi, j) for qi in range(n_tiles) for j in range(qi, -1, -1)]
    qh, scored, summed = {}, {}, {}
    state = {"acc": None, "carry": [None, None]}

    def score(qi, j):
        if j == qi:
            q = keys(q_ref, qi)
            zero = jnp.zeros_like(q)
            qh[qi] = (jnp.where(low, q, zero), jnp.where(low, zero, q))
        kt = keys(k_ref, j)
        out = []
        for h in range(2):
            z2 = _dot_nt(qh[qi][h], kt)
            if j == qi:
                parts = (softplus_split(z2[:half, :half], strictly_causal), softplus_split(z2[half:, :half]),
                         softplus_split(z2[half:, half:], strictly_causal))
                hi, lo = (_quadrants(*(p[i] for p in parts)) for i in range(2))
            else:
                hi, lo = softplus_split(z2)
            out.append((z2, hi, lo))
        scored[qi, j] = out

    def in_tile_sums(qi, j):
        summed[qi, j] = [(z2, _dot(hi, tri) + _dot(lo, tri)) for z2, hi, lo in scored.pop((qi, j))]

    def value_product(qi, j):
        if j == qi:
            state["acc"], state["carry"] = None, [None, None]
        carry = state["carry"]
        for h, (z2, csum) in enumerate(summed.pop((qi, j))):
            x = z2 - csum
            if j == qi:
                a = _quadrants(jnp.where(strictly_causal, jnp.exp2(x[:half, :half]), 0.0).astype(BF16),
                               jnp.exp2(x[half:, :half]).astype(BF16),
                               jnp.where(strictly_causal, jnp.exp2(x[half:, half:]), 0.0).astype(BF16))
            else:
                a = jnp.exp2(x - jnp.concatenate([carry[h], carry[h]], axis=1)).astype(BF16)
            if j > 0:
                total = jnp.broadcast_to(csum[:, 0:1], (ATT_TILE, LANES))
                carry[h] = total if carry[h] is None else carry[h] + total
            t = _dot(a, vh_ref[h, j * ATT_TILE:(j + 1) * ATT_TILE, :])
            state["acc"] = t if state["acc"] is None else state["acc"] + t
        if j == 0:
            o_ref[0, qi * ATT_TILE:(qi + 1) * ATT_TILE, :] = state["acc"].astype(BF16)

    for s in range(len(tiles) + 2):
        if s < len(tiles):
            score(*tiles[s])
        if 1 <= s <= len(tiles):
            in_tile_sums(*tiles[s - 1])
        if s >= 2:
            value_product(*tiles[s - 2])


def _stick_breaking(qkv, tri, batch, seq):
    qkv = qkv.reshape(batch, seq, 3 * SB_WIDTH)
    pairs = SB_WIDTH // LANES
    blk = (1, seq, LANES)
    return pl.pallas_call(
        _sb_kernel,
        grid=(batch, pairs),
        in_specs=[
            pl.BlockSpec(blk, lambda b, p: (b, 0, p)),
            pl.BlockSpec(blk, lambda b, p: (b, 0, pairs + p)),
            pl.BlockSpec(blk, lambda b, p: (b, 0, 2 * pairs + p)),
            _resident((ATT_TILE, ATT_TILE)),
        ],
        out_specs=pl.BlockSpec(blk, lambda b, p: (b, 0, p)),
        out_shape=jax.ShapeDtypeStruct((batch, seq, SB_WIDTH), BF16),
        scratch_shapes=[pltpu.VMEM((2, seq, LANES), BF16)],
        compiler_params=pltpu.CompilerParams(dimension_semantics=("arbitrary", "arbitrary"),
                                             vmem_limit_bytes=VMEM_LIMIT),
        name="stick_breaking_attention",
    )(qkv, qkv, qkv, tri)


def _diff_kernel(q_ref, k_ref, v_ref, lq1_ref, lk1_ref, lq2_ref, lk2_ref, gain_ref, o_ref):
    n_tiles = q_ref.shape[1] // ATT_TILE
    half = ATT_TILE // 2
    lam = (jnp.exp(jnp.sum(lq1_ref[...] * lk1_ref[...], keepdims=True))
           - jnp.exp(jnp.sum(lq2_ref[...] * lk2_ref[...], keepdims=True)) + LAMBDA_INIT)
    lane = lax.broadcasted_iota(jnp.int32, (ATT_TILE, LANES), 1)
    low = lane < HEAD_DIM
    row = lax.broadcasted_iota(jnp.int32, (half, half), 0)
    col = lax.broadcasted_iota(jnp.int32, (half, half), 1)
    causal = col <= row
    gain = gain_ref[...]

    def keys(ref, head, j):
        return ref[0, j * ATT_TILE:(j + 1) * ATT_TILE, head * LANES:(head + 1) * LANES]

    def scores(head, qi):
        q = keys(q_ref, head, qi)
        zero = jnp.zeros_like(q)
        return [[_dot_nt(qh, keys(k_ref, head, j)) for j in range(qi + 1)]
                for qh in (jnp.where(low, q, zero), jnp.where(low, zero, q))]

    def weigh(head, qi, zs):
        weights, row_scale = [], []
        for top in (True, False):
            rows = slice(0, half) if top else slice(half, ATT_TILE)
            e, denom = [], []
            for z in zs:
                pieces = [zt[rows, :] for zt in z[:qi]]
                diag = z[qi]
                if top:
                    pieces.append(jnp.where(causal, diag[:half, :half], -jnp.inf))
                else:
                    pieces += [diag[half:, :half], jnp.where(causal, diag[half:, half:], -jnp.inf)]
                m = _row_reduce(jnp.maximum, jnp.max, pieces)
                e.append([jnp.exp2(p - m) for p in pieces])
                denom.append(_row_reduce(jnp.add, jnp.sum, e[-1]))
            rho = lam * denom[0] / denom[1]
            weights.append([(e1 - rho * e2).astype(BF16) for e1, e2 in zip(*e)])
            row_scale.append(1.0 / denom[0])
        y = None
        for j in range(qi + 1):
            if j < qi:
                w = jnp.concatenate([weights[0][j], weights[1][j]], axis=0)
            else:
                w = _quadrants(weights[0][qi], weights[1][qi], weights[1][qi + 1])
            t = _dot(w, keys(v_ref, head, j))
            y = t if y is None else y + t
        y = y * jnp.concatenate(row_scale, axis=0)
        y = y * lax.rsqrt(jnp.mean(y * y, axis=-1, keepdims=True) + RMS_EPS) * gain
        o_ref[0, qi * ATT_TILE:(qi + 1) * ATT_TILE, head * LANES:(head + 1) * LANES] = (
            y * (1.0 - LAMBDA_INIT)).astype(BF16)

    heads = range(q_ref.shape[2] // LANES)
    pending = [scores(head, 0) for head in heads]
    for qi in range(n_tiles):
        following = [scores(head, qi + 1) if qi + 1 < n_tiles else None for head in heads]
        for head in heads:
            weigh(head, qi, pending[head])
        pending = following


def _differential(qkv, lq1, lk1, lq2, lk2, gain, batch, seq):
    qkv = qkv.reshape(batch, seq, 3 * DIFF_WIDTH)
    blk = (1, seq, DIFF_HEADS_PER_STEP * LANES)
    groups = DIFF_HEADS // DIFF_HEADS_PER_STEP
    return pl.pallas_call(
        _diff_kernel,
        grid=(batch, groups),
        in_specs=[
            pl.BlockSpec(blk, lambda b, h: (b, 0, h)),
            pl.BlockSpec(blk, lambda b, h: (b, 0, groups + h)),
            pl.BlockSpec(blk, lambda b, h: (b, 0, 2 * groups + h)),
            _resident((1, HEAD_DIM)), _resident((1, HEAD_DIM)), _resident((1, HEAD_DIM)), _resident((1, HEAD_DIM)),
            _resident((1, LANES)),
        ],
        out_specs=pl.BlockSpec(blk, lambda b, h: (b, 0, h)),
        out_shape=jax.ShapeDtypeStruct((batch, seq, DIFF_WIDTH), BF16),
        compiler_params=pltpu.CompilerParams(dimension_semantics=("arbitrary", "arbitrary"),
                                             vmem_limit_bytes=VMEM_LIMIT),
        name="differential_attention",
    )(qkv, qkv, qkv, lq1, lk1, lq2, lk2, gain)


def _merge_kernel(x_ref, ya_ref, yb_ref, wga_ref, wgb_ref, wa_ref, wb_ref, wo_ref, gain_ref, bias_ref, o_ref):
    gain, bias = gain_ref[...], bias_ref[...]
    n_chunks = D_MODEL // FF_CHUNK
    units = [(r, c) for r in range(ROW_TILE // SUB_ROWS) for c in range(n_chunks)]
    lhs, pre, acc = {}, {}, {}

    def branches(r, c):
        rows = slice(r * SUB_ROWS, (r + 1) * SUB_ROWS)
        cols = slice(c * FF_CHUNK, (c + 1) * FF_CHUNK)
        if c == 0:
            lhs[r] = (x_ref[rows, :].astype(BF16), ya_ref[rows, :], yb_ref[rows, :])
        xb, ya, yb = lhs[r]
        pre[r, c] = (_dot(xb, wga_ref[:, cols]), _dot(xb, wgb_ref[:, cols]),
                     _dot(ya, wa_ref[:, cols]), _dot(yb, wb_ref[:, cols]))

    def project(r, c):
        gate_a, gate_b, branch_a, branch_b = pre.pop((r, c))
        merged = jax.nn.sigmoid(gate_a) * branch_a + jax.nn.sigmoid(gate_b) * branch_b
        t = _dot(merged.astype(BF16), wo_ref[c * FF_CHUNK:(c + 1) * FF_CHUNK, :])
        acc[r] = t if c == 0 else acc[r] + t
        if c == n_chunks - 1:
            rows = slice(r * SUB_ROWS, (r + 1) * SUB_ROWS)
            o_ref[rows, :] = _layer_norm(DEEPNORM_ALPHA * x_ref[rows, :] + acc.pop(r), gain, bias)

    for s in range(len(units) + 1):
        if s < len(units):
            branches(*units[s])
        if s >= 1:
            project(*units[s - 1])


def _merge(x2d, ya, yb, w_gate_a, w_gate_b, w_a, w_b, w_o, gain, bias):
    tokens = x2d.shape[0]
    rows = lambda width: pl.BlockSpec((ROW_TILE, width), lambda i: (i, 0))
    return pl.pallas_call(
        _merge_kernel,
        grid=(tokens // ROW_TILE,),
        in_specs=[
            rows(D_MODEL), rows(SB_WIDTH), rows(DIFF_WIDTH),
            _resident((D_MODEL, D_MODEL)), _resident((D_MODEL, D_MODEL)),
            _resident((SB_WIDTH, D_MODEL)), _resident((DIFF_WIDTH, D_MODEL)),
            _resident((D_MODEL, D_MODEL)), _resident((1, D_MODEL)), _resident((1, D_MODEL)),
        ],
        out_specs=rows(D_MODEL),
        out_shape=jax.ShapeDtypeStruct((tokens, D_MODEL), F32),
        compiler_params=pltpu.CompilerParams(dimension_semantics=("arbitrary",), vmem_limit_bytes=VMEM_LIMIT),
        name="gated_merge_layernorm",
    )(x2d, ya, yb, w_gate_a, w_gate_b, w_a, w_b, w_o, gain, bias)


def _mlp_kernel(x_ref, w1_ref, w2_ref, gain_ref, bias_ref, o_ref):
    gain, bias = gain_ref[...], bias_ref[...]
    n_chunks = D_FF // FF_CHUNK
    units = [(r, c) for r in range(ROW_TILE // SUB_ROWS) for c in range(n_chunks)]
    xb, hidden, acc = {}, {}, {}

    def up(r, c):
        if c == 0:
            xb[r] = x_ref[r * SUB_ROWS:(r + 1) * SUB_ROWS, :].astype(BF16)
        hidden[r, c] = _dot(xb[r], w1_ref[:, c * FF_CHUNK:(c + 1) * FF_CHUNK])

    def down(r, c):
        h = jnp.maximum(hidden.pop((r, c)), 0.0)
        t = _dot((h * h).astype(BF16), w2_ref[c * FF_CHUNK:(c + 1) * FF_CHUNK, :])
        acc[r] = t if c == 0 else acc[r] + t
        if c == n_chunks - 1:
            rows = slice(r * SUB_ROWS, (r + 1) * SUB_ROWS)
            o_ref[rows, :] = _layer_norm(DEEPNORM_ALPHA * x_ref[rows, :] + acc.pop(r), gain, bias)

    for s in range(len(units) + 1):
        if s < len(units):
            up(*units[s])
        if s >= 1:
            down(*units[s - 1])


def _mlp(x2d, w1, w2, gain, bias):
    tokens = x2d.shape[0]
    rows = pl.BlockSpec((ROW_TILE, D_MODEL), lambda i: (i, 0))
    return pl.pallas_call(
        _mlp_kernel,
        grid=(tokens // ROW_TILE,),
        in_specs=[rows, _resident((D_MODEL, D_FF)), _resident((D_FF, D_MODEL)),
                  _resident((1, D_MODEL)), _resident((1, D_MODEL))],
        out_specs=rows,
        out_shape=jax.ShapeDtypeStruct((tokens, D_MODEL), F32),
        compiler_params=pltpu.CompilerParams(dimension_semantics=("arbitrary",), vmem_limit_bytes=VMEM_LIMIT),
        name="mlp_layernorm",
    )(x2d, w1, w2, gain, bias)


def _rope_tables(seq):
    inv_freq = ROPE_THETA ** (-jnp.arange(0, HEAD_DIM, 2, dtype=F32) / HEAD_DIM)
    ang = jnp.arange(seq, dtype=F32)[:, None] * inv_freq[None, :]
    cos, sin = jnp.cos(ang), jnp.sin(ang)
    return jnp.tile(cos, (1, 4)), jnp.tile(jnp.concatenate([-sin, sin], axis=1), (1, 2))


def kernel(x, w_in, w_branch_a, w_branch_b, w_out, lambda_q1, lambda_k1, lambda_q2, lambda_k2, subln_gain,
           ln1_gain, ln1_bias, w_ff1, w_ff2, ln2_gain, ln2_bias):
    batch, seq, _ = x.shape
    assert seq % ROW_TILE == 0 and seq % ATT_TILE == 0 and w_in.shape[0] == DEPTH
    cos_t, sin_t = _rope_tables(seq)
    tri = jnp.asarray(np.tril(np.ones((ATT_TILE, ATT_TILE), np.float32)), BF16)
    x2d = x.reshape(batch * seq, D_MODEL)
    for l in range(DEPTH):
        later = (w_branch_a[l], w_branch_b[l], w_out[l], w_ff1[l], w_ff2[l])
        qkv_a, qkv_b, w_gates, (w_a, w_b, w_o, w_1, w_2) = _project(x2d, w_in[l], later, cos_t, sin_t, seq)
        ya = _stick_breaking(qkv_a, tri, batch, seq).reshape(batch * seq, SB_WIDTH)
        yb = _differential(qkv_b, lambda_q1[l][None], lambda_k1[l][None], lambda_q2[l][None], lambda_k2[l][None],
                           subln_gain[l][None], batch, seq).reshape(batch * seq, DIFF_WIDTH)
        x2d = _merge(x2d, ya, yb, *w_gates, w_a, w_b, w_o, ln1_gain[l][None], ln1_bias[l][None])
        x2d = _mlp(x2d, w_1, w_2, ln2_gain[l][None], ln2_bias[l][None])
    return x2d.reshape(batch, seq, D_MODEL)
```

```python
import functools
import math

import numpy as np
import jax
import jax.numpy as jnp
from jax import lax
from jax.experimental import pallas as pl
from jax.experimental.pallas import tpu as pltpu

D_MODEL = 1024
HEAD_DIM = 64
SB_HEADS = 8
DIFF_HEADS = 4
SB_WIDTH = SB_HEADS * HEAD_DIM
DIFF_WIDTH = DIFF_HEADS * 2 * HEAD_DIM
QKV_WIDTH = 3 * SB_WIDTH + 3 * DIFF_WIDTH
D_FF = 4 * D_MODEL
ROPE_THETA = 10000.0
LN_EPS = 1e-5
RMS_EPS = 1e-5
DEPTH = 1
DEEPNORM_ALPHA = (2.0 * DEPTH) ** 0.25
LAMBDA_INIT = 0.8 - 0.6 * math.exp(-0.3 * 0)
LOG2E = math.log2(math.e)
SOFTPLUS_CLAMP = 64.0
QK_SCALE = HEAD_DIM ** -0.5 * LOG2E

LANES = 128
BF16_SUBLANES = 16
ROW_TILE = 1024
SUB_ROWS = 512
ATT_TILE = 256
DIFF_HEADS_PER_STEP = 2
FF_CHUNK = 512
VMEM_LIMIT = 56 * 1024 * 1024

F32 = jnp.float32
BF16 = jnp.bfloat16


def _dot(a, b):
    return lax.dot_general(a, b, (((1,), (0,)), ((), ())), preferred_element_type=F32)


def _dot_nt(a, b):
    return lax.dot_general(a, b, (((1,), (1,)), ((), ())), preferred_element_type=F32)


def _resident(shape):
    return pl.BlockSpec(shape, lambda *_: (0,) * len(shape), pipeline_mode=pl.Buffered(1))


def _row_reduce(elementwise, lane_reduce, pieces):
    chunks = [p[:, c:c + LANES] for p in pieces for c in range(0, p.shape[1], LANES)]
    return lane_reduce(functools.reduce(elementwise, chunks), axis=1, keepdims=True)


def _quadrants(top_left, bottom_left, bottom_right):
    top = jnp.concatenate([top_left, jnp.zeros_like(top_left)], axis=1)
    return jnp.concatenate([top, jnp.concatenate([bottom_left, bottom_right], axis=1)], axis=0)


def _layer_norm(r, gain, bias):
    mu = jnp.mean(r, axis=-1, keepdims=True)
    c = r - mu
    var = jnp.mean(c * c, axis=-1, keepdims=True)
    return c * lax.rsqrt(var + LN_EPS) * gain + bias


def _proj_kernel(x_ref, w_ref, cos_ref, sin_ref, *refs):
    n_cast = (len(refs) - 2) // 2
    cast_in, (oa_ref, ob_ref), cast_out = refs[:n_cast], refs[n_cast:n_cast + 2], refs[n_cast + 2:]
    for src, dst in zip(cast_in, cast_out):
        dst[...] = src[...].astype(BF16)

    xb = x_ref[...].astype(BF16)
    lane = lax.broadcasted_iota(jnp.int32, (ROW_TILE, LANES), 1)
    first_half = (lane % HEAD_DIM) < (HEAD_DIM // 2)
    cos = cos_ref[...]
    sin = sin_ref[...]

    def rope(t):
        swapped = jnp.where(first_half, pltpu.roll(t, LANES - HEAD_DIM // 2, 1), pltpu.roll(t, HEAD_DIM // 2, 1))
        return t * cos + swapped * sin

    n_sec = SB_WIDTH // LANES
    for sec in range(6):
        acc = _dot(xb, w_ref[:, sec * SB_WIDTH:(sec + 1) * SB_WIDTH].astype(BF16))
        out_ref = oa_ref if sec < 3 else ob_ref
        col0 = (sec % 3) * SB_WIDTH
        for g in range(n_sec):
            t = acc[:, g * LANES:(g + 1) * LANES]
            if sec in (3, 4):
                t = rope(t)
            if sec in (0, 3):
                t = t * QK_SCALE
            out_ref[:, col0 + g * LANES:col0 + (g + 1) * LANES] = t.astype(BF16)


def _project(x2d, w_in, later_weights, cos_t, sin_t, seq):
    tokens = x2d.shape[0]
    steps = tokens // ROW_TILE
    pos_blocks = seq // ROW_TILE
    gate_blocks = [QKV_WIDTH // D_MODEL + c for c in range((w_in.shape[1] - QKV_WIDTH) // D_MODEL)]
    cast_arrays = [w_in] * len(gate_blocks) + list(later_weights)
    cast_in, cast_out, cast_shapes = [], [], []
    for n, w in enumerate(cast_arrays):
        rows = w.shape[0] // steps
        assert rows * steps == w.shape[0] and rows % BF16_SUBLANES == 0
        width = D_MODEL if n < len(gate_blocks) else w.shape[1]
        col = gate_blocks[n] if n < len(gate_blocks) else 0
        cast_in.append(pl.BlockSpec((rows, width), lambda i, col=col: (i, col)))
        cast_out.append(pl.BlockSpec((rows, width), lambda i: (i, 0)))
        cast_shapes.append(jax.ShapeDtypeStruct((w.shape[0], width), BF16))
    outs = pl.pallas_call(
        _proj_kernel,
        grid=(steps,),
        in_specs=[
            pl.BlockSpec((ROW_TILE, D_MODEL), lambda i: (i, 0)),
            pl.BlockSpec((D_MODEL, QKV_WIDTH), lambda i: (0, 0), pipeline_mode=pl.Buffered(1)),
            pl.BlockSpec((ROW_TILE, LANES), lambda i: (i % pos_blocks, 0)),
            pl.BlockSpec((ROW_TILE, LANES), lambda i: (i % pos_blocks, 0)),
        ] + cast_in,
        out_specs=[
            pl.BlockSpec((ROW_TILE, 3 * SB_WIDTH), lambda i: (i, 0)),
            pl.BlockSpec((ROW_TILE, 3 * DIFF_WIDTH), lambda i: (i, 0)),
        ] + cast_out,
        out_shape=[
            jax.ShapeDtypeStruct((tokens, 3 * SB_WIDTH), BF16),
            jax.ShapeDtypeStruct((tokens, 3 * DIFF_WIDTH), BF16),
        ] + cast_shapes,
        compiler_params=pltpu.CompilerParams(dimension_semantics=("arbitrary",), vmem_limit_bytes=VMEM_LIMIT),
        name="qkv_projection",
    )(x2d, w_in, cos_t, sin_t, *cast_arrays)
    return outs[0], outs[1], outs[2:2 + len(gate_blocks)], outs[2 + len(gate_blocks):]


def _sb_kernel(q_ref, k_ref, v_ref, tri_ref, o_ref, vh_ref):
    n_tiles = q_ref.shape[1] // ATT_TILE
    half = ATT_TILE // 2
    tri = tri_ref[...]
    lane = lax.broadcasted_iota(jnp.int32, (ATT_TILE, LANES), 1)
    low = lane < HEAD_DIM
    row = lax.broadcasted_iota(jnp.int32, (half, half), 0)
    col = lax.broadcasted_iota(jnp.int32, (half, half), 1)
    strictly_causal = col < row

    def keys(ref, j):
        return ref[0, j * ATT_TILE:(j + 1) * ATT_TILE, :]

    for j in range(n_tiles):
        vt = keys(v_ref, j)
        zero = jnp.zeros_like(vt)
        vh_ref[0, j * ATT_TILE:(j + 1) * ATT_TILE, :] = jnp.where(low, vt, zero)
        vh_ref[1, j * ATT_TILE:(j + 1) * ATT_TILE, :] = jnp.where(low, zero, vt)

    def softplus_split(z2, mask=None):
        sp = jnp.maximum(z2, jnp.log(1.0 + jnp.exp2(jnp.minimum(z2, SOFTPLUS_CLAMP))) * LOG2E)
        if mask is not None:
            sp = jnp.where(mask, sp, 0.0)
        hi = lax.bitcast_convert_type(lax.bitcast_convert_type(sp, jnp.uint32) & jnp.uint32(0xFFFF0000), F32)
        return hi, sp - hi

    tiles = [(qi, j) for qi in range(n_tiles) for j in range(qi, -1, -1)]
    qh, scored, summed = {}, {}, {}
    state = {"acc": None, "carry": [None, None]}

    def score(qi, j):
        if j == qi:
            q = keys(q_ref, qi)
            zero = jnp.zeros_like(q)
            qh[qi] = (jnp.where(low, q, zero), jnp.where(low, zero, q))
        kt = keys(k_ref, j)
        out = []
        for h in range(2):
            z2 = _dot_nt(qh[qi][h], kt)
            if j == qi:
                parts = (softplus_split(z2[:half, :half], strictly_causal), softplus_split(z2[half:, :half]),
                         softplus_split(z2[half:, half:], strictly_causal))
                hi, lo = (_quadrants(*(p[i] for p in parts)) for i in range(2))
            else:
                hi, lo = softplus_split(z2)
            out.append((z2, hi, lo))
        scored[qi, j] = out

    def in_tile_sums(qi, j):
        summed[qi, j] = [(z2, _dot(hi, tri) + _dot(lo, tri)) for z2, hi, lo in scored.pop((qi, j))]

    def value_product(qi, j):
        if j == qi:
            state["acc"], state["carry"] = None, [None, None]
        carry = state["carry"]
        for h, (z2, csum) in enumerate(summed.pop((qi, j))):
            x = z2 - csum
            if j == qi:
                a = _quadrants(jnp.where(strictly_causal, jnp.exp2(x[:half, :half]), 0.0), jnp.exp2(x[half:, :half]),
                               jnp.where(strictly_causal, jnp.exp2(x[half:, half:]), 0.0))
            else:
                a = jnp.exp2(x - jnp.concatenate([carry[h], carry[h]], axis=1))
            if j > 0:
                total = jnp.broadcast_to(csum[:, 0:1], (ATT_TILE, LANES))
                carry[h] = total if carry[h] is None else carry[h] + total
            t = _dot(a, vh_ref[h, j * ATT_TILE:(j + 1) * ATT_TILE, :])
            state["acc"] = t if state["acc"] is None else state["acc"] + t
        if j == 0:
            o_ref[0, qi * ATT_TILE:(qi + 1) * ATT_TILE, :] = state["acc"].astype(BF16)

    for s in range(len(tiles) + 2):
        if s < len(tiles):
            score(*tiles[s])
        if 1 <= s <= len(tiles):
            in_tile_sums(*tiles[s - 1])
        if s >= 2:
            value_product(*tiles[s - 2])


def _stick_breaking(qkv, tri, batch, seq):
    qkv = qkv.reshape(batch, seq, 3 * SB_WIDTH)
    pairs = SB_WIDTH // LANES
    blk = (1, seq, LANES)
    return pl.pallas_call(
        _sb_kernel,
        grid=(batch, pairs),
        in_specs=[
            pl.BlockSpec(blk, lambda b, p: (b, 0, p)),
            pl.BlockSpec(blk, lambda b, p: (b, 0, pairs + p)),
            pl.BlockSpec(blk, lambda b, p: (b, 0, 2 * pairs + p)),
            _resident((ATT_TILE, ATT_TILE)),
        ],
        out_specs=pl.BlockSpec(blk, lambda b, p: (b, 0, p)),
        out_shape=jax.ShapeDtypeStruct((batch, seq, SB_WIDTH), BF16),
        scratch_shapes=[pltpu.VMEM((2, seq, LANES), BF16)],
        compiler_params=pltpu.CompilerParams(dimension_semantics=("arbitrary", "arbitrary"),
                                             vmem_limit_bytes=VMEM_LIMIT),
        name="stick_breaking_attention",
    )(qkv, qkv, qkv, tri)


def _diff_kernel(q_ref, k_ref, v_ref, lq1_ref, lk1_ref, lq2_ref, lk2_ref, gain_ref, o_ref):
    n_tiles = q_ref.shape[1] // ATT_TILE
    half = ATT_TILE // 2
    lam = (jnp.exp(jnp.sum(lq1_ref[...] * lk1_ref[...], keepdims=True))
           - jnp.exp(jnp.sum(lq2_ref[...] * lk2_ref[...], keepdims=True)) + LAMBDA_INIT)
    lane = lax.broadcasted_iota(jnp.int32, (ATT_TILE, LANES), 1)
    low = lane < HEAD_DIM
    row = lax.broadcasted_iota(jnp.int32, (half, half), 0)
    col = lax.broadcasted_iota(jnp.int32, (half, half), 1)
    causal = col <= row
    gain = gain_ref[...]

    def keys(ref, head, j):
        return ref[0, j * ATT_TILE:(j + 1) * ATT_TILE, head * LANES:(head + 1) * LANES]

    def scores(head, qi):
        q = keys(q_ref, head, qi)
        zero = jnp.zeros_like(q)
        return [[_dot_nt(qh, keys(k_ref, head, j)) for j in range(qi + 1)]
                for qh in (jnp.where(low, q, zero), jnp.where(low, zero, q))]

    def weigh(head, qi, zs):
        weights, row_scale = [], []
        for top in (True, False):
            rows = slice(0, half) if top else slice(half, ATT_TILE)
            e, denom = [], []
            for z in zs:
                pieces = [zt[rows, :] for zt in z[:qi]]
                diag = z[qi]
                if top:
                    pieces.append(jnp.where(causal, diag[:half, :half], -jnp.inf))
                else:
                    pieces += [diag[half:, :half], jnp.where(causal, diag[half:, half:], -jnp.inf)]
                m = _row_reduce(jnp.maximum, jnp.max, pieces)
                e.append([jnp.exp2(p - m) for p in pieces])
                denom.append(_row_reduce(jnp.add, jnp.sum, e[-1]))
            rho = lam * denom[0] / denom[1]
            weights.append([e1 - rho * e2 for e1, e2 in zip(*e)])
            row_scale.append(1.0 / denom[0])
        y = None
        for j in range(qi + 1):
            if j < qi:
                w = jnp.concatenate([weights[0][j], weights[1][j]], axis=0)
            else:
                w = _quadrants(weights[0][qi], weights[1][qi], weights[1][qi + 1])
            t = _dot(w, keys(v_ref, head, j))
            y = t if y is None else y + t
        y = y * jnp.concatenate(row_scale, axis=0)
        y = y * lax.rsqrt(jnp.mean(y * y, axis=-1, keepdims=True) + RMS_EPS) * gain
        o_ref[0, qi * ATT_TILE:(qi + 1) * ATT_TILE, head * LANES:(head + 1) * LANES] = (
            y * (1.0 - LAMBDA_INIT)).astype(BF16)

    heads = range(q_ref.shape[2] // LANES)
    pending = [scores(head, 0) for head in heads]
    for qi in range(n_tiles):
        following = [scores(head, qi + 1) if qi + 1 < n_tiles else None for head in heads]
        for head in heads:
            weigh(head, qi, pending[head])
        pending = following


def _differential(qkv, lq1, lk1, lq2, lk2, gain, batch, seq):
    qkv = qkv.reshape(batch, seq, 3 * DIFF_WIDTH)
    blk = (1, seq, DIFF_HEADS_PER_STEP * LANES)
    groups = DIFF_HEADS // DIFF_HEADS_PER_STEP
    return pl.pallas_call(
        _diff_kernel,
        grid=(batch, groups),
        in_specs=[
            pl.BlockSpec(blk, lambda b, h: (b, 0, h)),
            pl.BlockSpec(blk, lambda b, h: (b, 0, groups + h)),
            pl.BlockSpec(blk, lambda b, h: (b, 0, 2 * groups + h)),
            _resident((1, HEAD_DIM)), _resident((1, HEAD_DIM)), _resident((1, HEAD_DIM)), _resident((1, HEAD_DIM)),
            _resident((1, LANES)),
        ],
        out_specs=pl.BlockSpec(blk, lambda b, h: (b, 0, h)),
        out_shape=jax.ShapeDtypeStruct((batch, seq, DIFF_WIDTH), BF16),
        compiler_params=pltpu.CompilerParams(dimension_semantics=("arbitrary", "arbitrary"),
                                             vmem_limit_bytes=VMEM_LIMIT),
        name="differential_attention",
    )(qkv, qkv, qkv, lq1, lk1, lq2, lk2, gain)


def _merge_kernel(x_ref, ya_ref, yb_ref, wga_ref, wgb_ref, wa_ref, wb_ref, wo_ref, gain_ref, bias_ref, o_ref):
    gain, bias = gain_ref[...], bias_ref[...]
    n_chunks = D_MODEL // FF_CHUNK
    units = [(r, c) for r in range(ROW_TILE // SUB_ROWS) for c in range(n_chunks)]
    lhs, pre, acc = {}, {}, {}

    def branches(r, c):
        rows = slice(r * SUB_ROWS, (r + 1) * SUB_ROWS)
        cols = slice(c * FF_CHUNK, (c + 1) * FF_CHUNK)
        if c == 0:
            lhs[r] = (x_ref[rows, :].astype(BF16), ya_ref[rows, :], yb_ref[rows, :])
        xb, ya, yb = lhs[r]
        pre[r, c] = (_dot(xb, wga_ref[:, cols]), _dot(xb, wgb_ref[:, cols]),
                     _dot(ya, wa_ref[:, cols]), _dot(yb, wb_ref[:, cols]))

    def project(r, c):
        gate_a, gate_b, branch_a, branch_b = pre.pop((r, c))
        merged = jax.nn.sigmoid(gate_a) * branch_a + jax.nn.sigmoid(gate_b) * branch_b
        t = _dot(merged.astype(BF16), wo_ref[c * FF_CHUNK:(c + 1) * FF_CHUNK, :])
        acc[r] = t if c == 0 else acc[r] + t
        if c == n_chunks - 1:
            rows = slice(r * SUB_ROWS, (r + 1) * SUB_ROWS)
            o_ref[rows, :] = _layer_norm(DEEPNORM_ALPHA * x_ref[rows, :] + acc.pop(r), gain, bias)

    for s in range(len(units) + 1):
        if s < len(units):
            branches(*units[s])
        if s >= 1:
            project(*units[s - 1])


def _merge(x2d, ya, yb, w_gate_a, w_gate_b, w_a, w_b, w_o, gain, bias):
    tokens = x2d.shape[0]
    rows = lambda width: pl.BlockSpec((ROW_TILE, width), lambda i: (i, 0))
    return pl.pallas_call(
        _merge_kernel,
        grid=(tokens // ROW_TILE,),
        in_specs=[
            rows(D_MODEL), rows(SB_WIDTH), rows(DIFF_WIDTH),
            _resident((D_MODEL, D_MODEL)), _resident((D_MODEL, D_MODEL)),
            _resident((SB_WIDTH, D_MODEL)), _resident((DIFF_WIDTH, D_MODEL)),
            _resident((D_MODEL, D_MODEL)), _resident((1, D_MODEL)), _resident((1, D_MODEL)),
        ],
        out_specs=rows(D_MODEL),
        out_shape=jax.ShapeDtypeStruct((tokens, D_MODEL), F32),
        compiler_params=pltpu.CompilerParams(dimension_semantics=("arbitrary",), vmem_limit_bytes=VMEM_LIMIT),
        name="gated_merge_layernorm",
    )(x2d, ya, yb, w_gate_a, w_gate_b, w_a, w_b, w_o, gain, bias)


def _mlp_kernel(x_ref, w1_ref, w2_ref, gain_ref, bias_ref, o_ref):
    gain, bias = gain_ref[...], bias_ref[...]
    n_chunks = D_FF // FF_CHUNK
    units = [(r, c) for r in range(ROW_TILE // SUB_ROWS) for c in range(n_chunks)]
    xb, hidden, acc = {}, {}, {}

    def up(r, c):
        if c == 0:
            xb[r] = x_ref[r * SUB_ROWS:(r + 1) * SUB_ROWS, :].astype(BF16)
        hidden[r, c] = _dot(xb[r], w1_ref[:, c * FF_CHUNK:(c + 1) * FF_CHUNK])

    def down(r, c):
        h = jnp.maximum(hidden.pop((r, c)), 0.0)
        t = _dot((h * h).astype(BF16), w2_ref[c * FF_CHUNK:(c + 1) * FF_CHUNK, :])
        acc[r] = t if c == 0 else acc[r] + t
        if c == n_chunks - 1:
            rows = slice(r * SUB_ROWS, (r + 1) * SUB_ROWS)
            o_ref[rows, :] = _layer_norm(DEEPNORM_ALPHA * x_ref[rows, :] + acc.pop(r), gain, bias)

    for s in range(len(units) + 1):
        if s < len(units):
            up(*units[s])
        if s >= 1:
            down(*units[s - 1])


def _mlp(x2d, w1, w2, gain, bias):
    tokens = x2d.shape[0]
    rows = pl.BlockSpec((ROW_TILE, D_MODEL), lambda i: (i, 0))
    return pl.pallas_call(
        _mlp_kernel,
        grid=(tokens // ROW_TILE,),
        in_specs=[rows, _resident((D_MODEL, D_FF)), _resident((D_FF, D_MODEL)),
                  _resident((1, D_MODEL)), _resident((1, D_MODEL))],
        out_specs=rows,
        out_shape=jax.ShapeDtypeStruct((tokens, D_MODEL), F32),
        compiler_params=pltpu.CompilerParams(dimension_semantics=("arbitrary",), vmem_limit_bytes=VMEM_LIMIT),
        name="mlp_layernorm",
    )(x2d, w1, w2, gain, bias)


def _rope_tables(seq):
    inv_freq = ROPE_THETA ** (-jnp.arange(0, HEAD_DIM, 2, dtype=F32) / HEAD_DIM)
    ang = jnp.arange(seq, dtype=F32)[:, None] * inv_freq[None, :]
    cos, sin = jnp.cos(ang), jnp.sin(ang)
    return jnp.tile(cos, (1, 4)), jnp.tile(jnp.concatenate([-sin, sin], axis=1), (1, 2))


def kernel(x, w_in, w_branch_a, w_branch_b, w_out, lambda_q1, lambda_k1, lambda_q2, lambda_k2, subln_gain,
           ln1_gain, ln1_bias, w_ff1, w_ff2, ln2_gain, ln2_bias):
    batch, seq, _ = x.shape
    assert seq % ROW_TILE == 0 and seq % ATT_TILE == 0 and w_in.shape[0] == DEPTH
    cos_t, sin_t = _rope_tables(seq)
    tri = jnp.asarray(np.tril(np.ones((ATT_TILE, ATT_TILE), np.float32)), BF16)
    x2d = x.reshape(batch * seq, D_MODEL)
    for l in range(DEPTH):
        later = (w_branch_a[l], w_branch_b[l], w_out[l], w_ff1[l], w_ff2[l])
        qkv_a, qkv_b, w_gates, (w_a, w_b, w_o, w_1, w_2) = _project(x2d, w_in[l], later, cos_t, sin_t, seq)
        ya = _stick_breaking(qkv_a, tri, batch, seq).reshape(batch * seq, SB_WIDTH)
        yb = _differential(qkv_b, lambda_q1[l][None], lambda_k1[l][None], lambda_q2[l][None], lambda_k2[l][None],
                           subln_gain[l][None], batch, seq).reshape(batch * seq, DIFF_WIDTH)
        x2d = _merge(x2d, ya, yb, *w_gates, w_a, w_b, w_o, ln1_gain[l][None], ln1_bias[l][None])
        x2d = _mlp(x2d, w_1, w_2, ln2_gain[l][None], ln2_bias[l][None])
    return x2d.reshape(batch, seq, D_MODEL)
```

```python
import functools
import math

import numpy as np
import jax
import jax.numpy as jnp
from jax import lax
from jax.experimental import pallas as pl
from jax.experimental.pallas import tpu as pltpu

D_MODEL = 1024
HEAD_DIM = 64
SB_HEADS = 8
DIFF_HEADS = 4
SB_WIDTH = SB_HEADS * HEAD_DIM
DIFF_WIDTH = DIFF_HEADS * 2 * HEAD_DIM
QKV_WIDTH = 3 * SB_WIDTH + 3 * DIFF_WIDTH
D_FF = 4 * D_MODEL
ROPE_THETA = 10000.0
LN_EPS = 1e-5
RMS_EPS = 1e-5
DEPTH = 1
DEEPNORM_ALPHA = (2.0 * DEPTH) ** 0.25
LAMBDA_INIT = 0.8 - 0.6 * math.exp(-0.3 * 0)
LOG2E = math.log2(math.e)
SOFTPLUS_CLAMP = 64.0
QK_SCALE = HEAD_DIM ** -0.5 * LOG2E

LANES = 128
BF16_SUBLANES = 16
ROW_TILE = 1024
SUB_ROWS = 256
ATT_TILE = 256
DIFF_HEADS_PER_STEP = 2
FF_CHUNK = 512
VMEM_LIMIT = 56 * 1024 * 1024

F32 = jnp.float32
BF16 = jnp.bfloat16


def _dot(a, b):
    return lax.dot_general(a, b, (((1,), (0,)), ((), ())), preferred_element_type=F32)


def _dot_nt(a, b):
    return lax.dot_general(a, b, (((1,), (1,)), ((), ())), preferred_element_type=F32)


def _resident(shape):
    return pl.BlockSpec(shape, lambda *_: (0,) * len(shape), pipeline_mode=pl.Buffered(1))


def _row_reduce(elementwise, lane_reduce, pieces):
    chunks = [p[:, c:c + LANES] for p in pieces for c in range(0, p.shape[1], LANES)]
    return lane_reduce(functools.reduce(elementwise, chunks), axis=1, keepdims=True)


def _quadrants(top_left, bottom_left, bottom_right):
    top = jnp.concatenate([top_left, jnp.zeros_like(top_left)], axis=1)
    return jnp.concatenate([top, jnp.concatenate([bottom_left, bottom_right], axis=1)], axis=0)


def _layer_norm(r, gain, bias):
    mu = jnp.mean(r, axis=-1, keepdims=True)
    c = r - mu
    var = jnp.mean(c * c, axis=-1, keepdims=True)
    return c * lax.rsqrt(var + LN_EPS) * gain + bias


def _proj_kernel(x_ref, w_ref, cos_ref, sin_ref, *refs):
    n_cast = (len(refs) - 2) // 2
    cast_in, (oa_ref, ob_ref), cast_out = refs[:n_cast], refs[n_cast:n_cast + 2], refs[n_cast + 2:]
    for src, dst in zip(cast_in, cast_out):
        dst[...] = src[...].astype(BF16)

    xb = x_ref[...].astype(BF16)
    lane = lax.broadcasted_iota(jnp.int32, (ROW_TILE, LANES), 1)
    first_half = (lane % HEAD_DIM) < (HEAD_DIM // 2)
    cos = cos_ref[...]
    sin = sin_ref[...]

    def rope(t):
        swapped = jnp.where(first_half, pltpu.roll(t, LANES - HEAD_DIM // 2, 1), pltpu.roll(t, HEAD_DIM // 2, 1))
        return t * cos + swapped * sin

    n_sec = SB_WIDTH // LANES
    for sec in range(6):
        acc = _dot(xb, w_ref[:, sec * SB_WIDTH:(sec + 1) * SB_WIDTH].astype(BF16))
        out_ref = oa_ref if sec < 3 else ob_ref
        col0 = (sec % 3) * SB_WIDTH
        for g in range(n_sec):
            t = acc[:, g * LANES:(g + 1) * LANES]
            if sec in (3, 4):
                t = rope(t)
            if sec in (0, 3):
                t = t * QK_SCALE
            out_ref[:, col0 + g * LANES:col0 + (g + 1) * LANES] = t.astype(BF16)


def _project(x2d, w_in, later_weights, cos_t, sin_t, seq):
    tokens = x2d.shape[0]
    steps = tokens // ROW_TILE
    pos_blocks = seq // ROW_TILE
    gate_blocks = [QKV_WIDTH // D_MODEL + c for c in range((w_in.shape[1] - QKV_WIDTH) // D_MODEL)]
    cast_arrays = [w_in] * len(gate_blocks) + list(later_weights)
    cast_in, cast_out, cast_shapes = [], [], []
    for n, w in enumerate(cast_arrays):
        rows = w.shape[0] // steps
        assert rows * steps == w.shape[0] and rows % BF16_SUBLANES == 0
        width = D_MODEL if n < len(gate_blocks) else w.shape[1]
        col = gate_blocks[n] if n < len(gate_blocks) else 0
        cast_in.append(pl.BlockSpec((rows, width), lambda i, col=col: (i, col)))
        cast_out.append(pl.BlockSpec((rows, width), lambda i: (i, 0)))
        cast_shapes.append(jax.ShapeDtypeStruct((w.shape[0], width), BF16))
    outs = pl.pallas_call(
        _proj_kernel,
        grid=(steps,),
        in_specs=[
            pl.BlockSpec((ROW_TILE, D_MODEL), lambda i: (i, 0)),
            pl.BlockSpec((D_MODEL, QKV_WIDTH), lambda i: (0, 0), pipeline_mode=pl.Buffered(1)),
            pl.BlockSpec((ROW_TILE, LANES), lambda i: (i % pos_blocks, 0)),
            pl.BlockSpec((ROW_TILE, LANES), lambda i: (i % pos_blocks, 0)),
        ] + cast_in,
        out_specs=[
            pl.BlockSpec((ROW_TILE, 3 * SB_WIDTH), lambda i: (i, 0)),
            pl.BlockSpec((ROW_TILE, 3 * DIFF_WIDTH), lambda i: (i, 0)),
        ] + cast_out,
        out_shape=[
            jax.ShapeDtypeStruct((tokens, 3 * SB_WIDTH), BF16),
            jax.ShapeDtypeStruct((tokens, 3 * DIFF_WIDTH), BF16),
        ] + cast_shapes,
        compiler_params=pltpu.CompilerParams(dimension_semantics=("arbitrary",), vmem_limit_bytes=VMEM_LIMIT),
        name="qkv_projection",
    )(x2d, w_in, cos_t, sin_t, *cast_arrays)
    return outs[0], outs[1], outs[2:2 + len(gate_blocks)], outs[2 + len(gate_blocks):]


def _sb_kernel(q_ref, k_ref, v_ref, tri_ref, o_ref, vh_ref):
    n_tiles = q_ref.shape[1] // ATT_TILE
    half = ATT_TILE // 2
    tri = tri_ref[...]
    lane = lax.broadcasted_iota(jnp.int32, (ATT_TILE, LANES), 1)
    low = lane < HEAD_DIM
    row = lax.broadcasted_iota(jnp.int32, (half, half), 0)
    col = lax.broadcasted_iota(jnp.int32, (half, half), 1)
    strictly_causal = col < row

    def keys(ref, j):
        return ref[0, j * ATT_TILE:(j + 1) * ATT_TILE, :]

    for j in range(n_tiles):
        vt = keys(v_ref, j)
        zero = jnp.zeros_like(vt)
        vh_ref[0, j * ATT_TILE:(j + 1) * ATT_TILE, :] = jnp.where(low, vt, zero)
        vh_ref[1, j * ATT_TILE:(j + 1) * ATT_TILE, :] = jnp.where(low, zero, vt)

    def softplus_split(z2, mask=None):
        sp = jnp.maximum(z2, jnp.log(1.0 + jnp.exp2(jnp.minimum(z2, SOFTPLUS_CLAMP))) * LOG2E)
        if mask is not None:
            sp = jnp.where(mask, sp, 0.0)
        hi = lax.bitcast_convert_type(lax.bitcast_convert_type(sp, jnp.uint32) & jnp.uint32(0xFFFF0000), F32)
        return hi, sp - hi

    tiles = [(qi, j) for qi in range(n_tiles) for j in range(qi, -1, -1)]
    qh, scored, summed = {}, {}, {}
    state = {"acc": None, "carry": [None, None]}

    def score(qi, j):
        if j == qi:
            q = keys(q_ref, qi)
            zero = jnp.zeros_like(q)
            qh[qi] = (jnp.where(low, q, zero), jnp.where(low, zero, q))
        kt = keys(k_ref, j)
        out = []
        for h in range(2):
            z2 = _dot_nt(qh[qi][h], kt)
            if j == qi:
                parts = (softplus_split(z2[:half, :half], strictly_causal), softplus_split(z2[half:, :half]),
                         softplus_split(z2[half:, half:], strictly_causal))
                hi, lo = (_quadrants(*(p[i] for p in parts)) for i in range(2))
            else:
                hi, lo = softplus_split(z2)
            out.append((z2, hi, lo))
        scored[qi, j] = out

    def in_tile_sums(qi, j):
        summed[qi, j] = [(z2, _dot(hi, tri) + _dot(lo, tri)) for z2, hi, lo in scored.pop((qi, j))]

    def value_product(qi, j):
        if j == qi:
            state["acc"], state["carry"] = None, [None, None]
        carry = state["carry"]
        for h, (z2, csum) in enumerate(summed.pop((qi, j))):
            x = z2 - csum
            if j == qi:
                a = _quadrants(jnp.where(strictly_causal, jnp.exp2(x[:half, :half]), 0.0), jnp.exp2(x[half:, :half]),
                               jnp.where(strictly_causal, jnp.exp2(x[half:, half:]), 0.0))
            else:
                a = jnp.exp2(x - jnp.concatenate([carry[h], carry[h]], axis=1))
            if j > 0:
                total = jnp.broadcast_to(csum[:, 0:1], (ATT_TILE, LANES))
                carry[h] = total if carry[h] is None else carry[h] + total
            t = _dot(a, vh_ref[h, j * ATT_TILE:(j + 1) * ATT_TILE, :])
            state["acc"] = t if state["acc"] is None else state["acc"] + t
        if j == 0:
            o_ref[0, qi * ATT_TILE:(qi + 1) * ATT_TILE, :] = state["acc"].astype(BF16)

    for s in range(len(tiles) + 2):
        if s < len(tiles):
            score(*tiles[s])
        if 1 <= s <= len(tiles):
            in_tile_sums(*tiles[s - 1])
        if s >= 2:
            value_product(*tiles[s - 2])


def _stick_breaking(qkv, tri, batch, seq):
    qkv = qkv.reshape(batch, seq, 3 * SB_WIDTH)
    pairs = SB_WIDTH // LANES
    blk = (1, seq, LANES)
    return pl.pallas_call(
        _sb_kernel,
        grid=(batch, pairs),
        in_specs=[
            pl.BlockSpec(blk, lambda b, p: (b, 0, p)),
            pl.BlockSpec(blk, lambda b, p: (b, 0, pairs + p)),
            pl.BlockSpec(blk, lambda b, p: (b, 0, 2 * pairs + p)),
            _resident((ATT_TILE, ATT_TILE)),
        ],
        out_specs=pl.BlockSpec(blk, lambda b, p: (b, 0, p)),
        out_shape=jax.ShapeDtypeStruct((batch, seq, SB_WIDTH), BF16),
        scratch_shapes=[pltpu.VMEM((2, seq, LANES), BF16)],
        compiler_params=pltpu.CompilerParams(dimension_semantics=("arbitrary", "arbitrary"),
                                             vmem_limit_bytes=VMEM_LIMIT),
        name="stick_breaking_attention",
    )(qkv, qkv, qkv, tri)


def _diff_kernel(q_ref, k_ref, v_ref, lq1_ref, lk1_ref, lq2_ref, lk2_ref, gain_ref, o_ref):
    n_tiles = q_ref.shape[1] // ATT_TILE
    half = ATT_TILE // 2
    lam = (jnp.exp(jnp.sum(lq1_ref[...] * lk1_ref[...], keepdims=True))
           - jnp.exp(jnp.sum(lq2_ref[...] * lk2_ref[...], keepdims=True)) + LAMBDA_INIT)
    lane = lax.broadcasted_iota(jnp.int32, (ATT_TILE, LANES), 1)
    low = lane < HEAD_DIM
    row = lax.broadcasted_iota(jnp.int32, (half, half), 0)
    col = lax.broadcasted_iota(jnp.int32, (half, half), 1)
    causal = col <= row
    gain = gain_ref[...]

    def keys(ref, head, j):
        return ref[0, j * ATT_TILE:(j + 1) * ATT_TILE, head * LANES:(head + 1) * LANES]

    def scores(head, qi):
        q = keys(q_ref, head, qi)
        zero = jnp.zeros_like(q)
        return [[_dot_nt(qh, keys(k_ref, head, j)) for j in range(qi + 1)]
                for qh in (jnp.where(low, q, zero), jnp.where(low, zero, q))]

    def weigh(head, qi, zs):
        weights, row_scale = [], []
        for top in (True, False):
            rows = slice(0, half) if top else slice(half, ATT_TILE)
            e, denom = [], []
            for z in zs:
                pieces = [zt[rows, :] for zt in z[:qi]]
                diag = z[qi]
                if top:
                    pieces.append(jnp.where(causal, diag[:half, :half], -jnp.inf))
                else:
                    pieces += [diag[half:, :half], jnp.where(causal, diag[half:, half:], -jnp.inf)]
                m = _row_reduce(jnp.maximum, jnp.max, pieces)
                e.append([jnp.exp2(p - m) for p in pieces])
                denom.append(_row_reduce(jnp.add, jnp.sum, e[-1]))
            rho = lam * denom[0] / denom[1]
            weights.append([e1 - rho * e2 for e1, e2 in zip(*e)])
            row_scale.append(1.0 / denom[0])
        y = None
        for j in range(qi + 1):
            if j < qi:
                w = jnp.concatenate([weights[0][j], weights[1][j]], axis=0)
            else:
                w = _quadrants(weights[0][qi], weights[1][qi], weights[1][qi + 1])
            t = _dot(w, keys(v_ref, head, j))
            y = t if y is None else y + t
        y = y * jnp.concatenate(row_scale, axis=0)
        y = y * lax.rsqrt(jnp.mean(y * y, axis=-1, keepdims=True) + RMS_EPS) * gain
        o_ref[0, qi * ATT_TILE:(qi + 1) * ATT_TILE, head * LANES:(head + 1) * LANES] = (
            y * (1.0 - LAMBDA_INIT)).astype(BF16)

    heads = range(q_ref.shape[2] // LANES)
    pending = [scores(head, 0) for head in heads]
    for qi in range(n_tiles):
        following = [scores(head, qi + 1) if qi + 1 < n_tiles else None for head in heads]
        for head in heads:
            weigh(head, qi, pending[head])
        pending = following


def _differential(qkv, lq1, lk1, lq2, lk2, gain, batch, seq):
    qkv = qkv.reshape(batch, seq, 3 * DIFF_WIDTH)
    blk = (1, seq, DIFF_HEADS_PER_STEP * LANES)
    groups = DIFF_HEADS // DIFF_HEADS_PER_STEP
    return pl.pallas_call(
        _diff_kernel,
        grid=(batch, groups),
        in_specs=[
            pl.BlockSpec(blk, lambda b, h: (b, 0, h)),
            pl.BlockSpec(blk, lambda b, h: (b, 0, groups + h)),
            pl.BlockSpec(blk, lambda b, h: (b, 0, 2 * groups + h)),
            _resident((1, HEAD_DIM)), _resident((1, HEAD_DIM)), _resident((1, HEAD_DIM)), _resident((1, HEAD_DIM)),
            _resident((1, LANES)),
        ],
        out_specs=pl.BlockSpec(blk, lambda b, h: (b, 0, h)),
        out_shape=jax.ShapeDtypeStruct((batch, seq, DIFF_WIDTH), BF16),
        compiler_params=pltpu.CompilerParams(dimension_semantics=("arbitrary", "arbitrary"),
                                             vmem_limit_bytes=VMEM_LIMIT),
        name="differential_attention",
    )(qkv, qkv, qkv, lq1, lk1, lq2, lk2, gain)


def _merge_kernel(x_ref, ya_ref, yb_ref, wga_ref, wgb_ref, wa_ref, wb_ref, wo_ref, gain_ref, bias_ref, o_ref):
    gain, bias = gain_ref[...], bias_ref[...]
    n_chunks = D_MODEL // FF_CHUNK
    units = [(r, c) for r in range(ROW_TILE // SUB_ROWS) for c in range(n_chunks)]
    lhs, pre, acc = {}, {}, {}

    def branches(r, c):
        rows = slice(r * SUB_ROWS, (r + 1) * SUB_ROWS)
        cols = slice(c * FF_CHUNK, (c + 1) * FF_CHUNK)
        if c == 0:
            lhs[r] = (x_ref[rows, :].astype(BF16), ya_ref[rows, :], yb_ref[rows, :])
        xb, ya, yb = lhs[r]
        pre[r, c] = (_dot(xb, wga_ref[:, cols]), _dot(xb, wgb_ref[:, cols]),
                     _dot(ya, wa_ref[:, cols]), _dot(yb, wb_ref[:, cols]))

    def project(r, c):
        gate_a, gate_b, branch_a, branch_b = pre.pop((r, c))
        merged = jax.nn.sigmoid(gate_a) * branch_a + jax.nn.sigmoid(gate_b) * branch_b
        t = _dot(merged, wo_ref[c * FF_CHUNK:(c + 1) * FF_CHUNK, :])
        acc[r] = t if c == 0 else acc[r] + t
        if c == n_chunks - 1:
            rows = slice(r * SUB_ROWS, (r + 1) * SUB_ROWS)
            o_ref[rows, :] = _layer_norm(DEEPNORM_ALPHA * x_ref[rows, :] + acc.pop(r), gain, bias)

    for s in range(len(units) + 1):
        if s < len(units):
            branches(*units[s])
        if s >= 1:
            project(*units[s - 1])


def _merge(x2d, ya, yb, w_gate_a, w_gate_b, w_a, w_b, w_o, gain, bias):
    tokens = x2d.shape[0]
    rows = lambda width: pl.BlockSpec((ROW_TILE, width), lambda i: (i, 0))
    return pl.pallas_call(
        _merge_kernel,
        grid=(tokens // ROW_TILE,),
        in_specs=[
            rows(D_MODEL), rows(SB_WIDTH), rows(DIFF_WIDTH),
            _resident((D_MODEL, D_MODEL)), _resident((D_MODEL, D_MODEL)),
            _resident((SB_WIDTH, D_MODEL)), _resident((DIFF_WIDTH, D_MODEL)),
            _resident((D_MODEL, D_MODEL)), _resident((1, D_MODEL)), _resident((1, D_MODEL)),
        ],
        out_specs=rows(D_MODEL),
        out_shape=jax.ShapeDtypeStruct((tokens, D_MODEL), F32),
        compiler_params=pltpu.CompilerParams(dimension_semantics=("arbitrary",), vmem_limit_bytes=VMEM_LIMIT),
        name="gated_merge_layernorm",
    )(x2d, ya, yb, w_gate_a, w_gate_b, w_a, w_b, w_o, gain, bias)


def _mlp_kernel(x_ref, w1_ref, w2_ref, gain_ref, bias_ref, o_ref):
    gain, bias = gain_ref[...], bias_ref[...]
    n_chunks = D_FF // FF_CHUNK
    units = [(r, c) for r in range(ROW_TILE // SUB_ROWS) for c in range(n_chunks)]
    xb, hidden, acc = {}, {}, {}

    def up(r, c):
        if c == 0:
            xb[r] = x_ref[r * SUB_ROWS:(r + 1) * SUB_ROWS, :].astype(BF16)
        hidden[r, c] = _dot(xb[r], w1_ref[:, c * FF_CHUNK:(c + 1) * FF_CHUNK])

    def down(r, c):
        h = jnp.maximum(hidden.pop((r, c)), 0.0)
        t = _dot(h * h, w2_ref[c * FF_CHUNK:(c + 1) * FF_CHUNK, :])
        acc[r] = t if c == 0 else acc[r] + t
        if c == n_chunks - 1:
            rows = slice(r * SUB_ROWS, (r + 1) * SUB_ROWS)
            o_ref[rows, :] = _layer_norm(DEEPNORM_ALPHA * x_ref[rows, :] + acc.pop(r), gain, bias)

    for s in range(len(units) + 1):
        if s < len(units):
            up(*units[s])
        if s >= 1:
            down(*units[s - 1])


def _mlp(x2d, w1, w2, gain, bias):
    tokens = x2d.shape[0]
    rows = pl.BlockSpec((ROW_TILE, D_MODEL), lambda i: (i, 0))
    return pl.pallas_call(
        _mlp_kernel,
        grid=(tokens // ROW_TILE,),
        in_specs=[rows, _resident((D_MODEL, D_FF)), _resident((D_FF, D_MODEL)),
                  _resident((1, D_MODEL)), _resident((1, D_MODEL))],
        out_specs=rows,
        out_shape=jax.ShapeDtypeStruct((tokens, D_MODEL), F32),
        compiler_params=pltpu.CompilerParams(dimension_semantics=("arbitrary",), vmem_limit_bytes=VMEM_LIMIT),
        name="mlp_layernorm",
    )(x2d, w1, w2, gain, bias)


def _rope_tables(seq):
    inv_freq = ROPE_THETA ** (-jnp.arange(0, HEAD_DIM, 2, dtype=F32) / HEAD_DIM)
    ang = jnp.arange(seq, dtype=F32)[:, None] * inv_freq[None, :]
    cos, sin = jnp.cos(ang), jnp.sin(ang)
    return jnp.tile(cos, (1, 4)), jnp.tile(jnp.concatenate([-sin, sin], axis=1), (1, 2))


def kernel(x, w_in, w_branch_a, w_branch_b, w_out, lambda_q1, lambda_k1, lambda_q2, lambda_k2, subln_gain,
           ln1_gain, ln1_bias, w_ff1, w_ff2, ln2_gain, ln2_bias):
    batch, seq, _ = x.shape
    assert seq % ROW_TILE == 0 and seq % ATT_TILE == 0 and w_in.shape[0] == DEPTH
    cos_t, sin_t = _rope_tables(seq)
    tri = jnp.asarray(np.tril(np.ones((ATT_TILE, ATT_TILE), np.float32)), BF16)
    x2d = x.reshape(batch * seq, D_MODEL)
    for l in range(DEPTH):
        later = (w_branch_a[l], w_branch_b[l], w_out[l], w_ff1[l], w_ff2[l])
        qkv_a, qkv_b, w_gates, (w_a, w_b, w_o, w_1, w_2) = _project(x2d, w_in[l], later, cos_t, sin_t, seq)
        ya = _stick_breaking(qkv_a, tri, batch, seq).reshape(batch * seq, SB_WIDTH)
        yb = _differential(qkv_b, lambda_q1[l][None], lambda_k1[l][None], lambda_q2[l][None], lambda_k2[l][None],
                           subln_gain[l][None], batch, seq).reshape(batch * seq, DIFF_WIDTH)
        x2d = _merge(x2d, ya, yb, *w_gates, w_a, w_b, w_o, ln1_gain[l][None], ln1_bias[l][None])
        x2d = _mlp(x2d, w_1, w_2, ln2_gain[l][None], ln2_bias[l][None])
    return x2d.reshape(batch, seq, D_MODEL)
```

```python
import functools
import math

import numpy as np
import jax
import jax.numpy as jnp
from jax import lax
from jax.experimental import pallas as pl
from jax.experimental.pallas import tpu as pltpu

D_MODEL = 1024
HEAD_DIM = 64
SB_HEADS = 8
DIFF_HEADS = 4
SB_WIDTH = SB_HEADS * HEAD_DIM
DIFF_WIDTH = DIFF_HEADS * 2 * HEAD_DIM
QKV_WIDTH = 3 * SB_WIDTH + 3 * DIFF_WIDTH
D_FF = 4 * D_MODEL
ROPE_THETA = 10000.0
LN_EPS = 1e-5
RMS_EPS = 1e-5
DEPTH = 1
DEEPNORM_ALPHA = (2.0 * DEPTH) ** 0.25
LAMBDA_INIT = 0.8 - 0.6 * math.exp(-0.3 * 0)
LOG2E = math.log2(math.e)
SOFTPLUS_CLAMP = 64.0
QK_SCALE = HEAD_DIM ** -0.5 * LOG2E

LANES = 128
BF16_SUBLANES = 16
BF16_BITS = 0xFFFF0000
ROW_TILE = 1024
SUB_ROWS = 256
ATT_TILE = 256
DIFF_HEADS_PER_STEP = 2
SB_PAIRS_PER_STEP = 1
FF_CHUNK = 512
VMEM_LIMIT = 56 * 1024 * 1024

F32 = jnp.float32
BF16 = jnp.bfloat16


def _dot(a, b):
    return lax.dot_general(a, b, (((1,), (0,)), ((), ())), preferred_element_type=F32)


def _dot_nt(a, b):
    return lax.dot_general(a, b, (((1,), (1,)), ((), ())), preferred_element_type=F32)


def _resident(shape):
    return pl.BlockSpec(shape, lambda *_: (0,) * len(shape), pipeline_mode=pl.Buffered(1))


def _row_reduce(elementwise, lane_reduce, pieces):
    chunks = [p[:, c:c + LANES] for p in pieces for c in range(0, p.shape[1], LANES)]
    return lane_reduce(functools.reduce(elementwise, chunks), axis=1, keepdims=True)


def _quadrants(top_left, bottom_left, bottom_right):
    top = jnp.concatenate([top_left, jnp.zeros_like(top_left)], axis=1)
    return jnp.concatenate([top, jnp.concatenate([bottom_left, bottom_right], axis=1)], axis=0)


def _layer_norm(r, gain, bias):
    mu = jnp.mean(r, axis=-1, keepdims=True)
    c = r - mu
    var = jnp.mean(c * c, axis=-1, keepdims=True)
    return c * lax.rsqrt(var + LN_EPS) * gain + bias


def _proj_kernel(x_ref, w_ref, cos_ref, sin_ref, *refs):
    n_cast = (len(refs) - 2) // 2
    cast_in, (oa_ref, ob_ref), cast_out = refs[:n_cast], refs[n_cast:n_cast + 2], refs[n_cast + 2:]
    for src, dst in zip(cast_in, cast_out):
        dst[...] = src[...].astype(BF16)

    xb = x_ref[...].astype(BF16)
    lane = lax.broadcasted_iota(jnp.int32, (ROW_TILE, LANES), 1)
    first_half = (lane % HEAD_DIM) < (HEAD_DIM // 2)
    cos = cos_ref[...]
    sin = sin_ref[...]

    def rope(t):
        swapped = jnp.where(first_half, pltpu.roll(t, LANES - HEAD_DIM // 2, 1), pltpu.roll(t, HEAD_DIM // 2, 1))
        return t * cos + swapped * sin

    n_sec = SB_WIDTH // LANES
    for sec in range(QKV_WIDTH // SB_WIDTH):
        acc = _dot(xb, w_ref[:, sec * SB_WIDTH:(sec + 1) * SB_WIDTH].astype(BF16))
        out_ref = oa_ref if sec < 3 else ob_ref
        col0 = (sec % 3) * SB_WIDTH
        for g in range(n_sec):
            t = acc[:, g * LANES:(g + 1) * LANES]
            if sec in (3, 4):
                t = rope(t)
            if sec in (0, 3):
                t = t * QK_SCALE
            out_ref[:, col0 + g * LANES:col0 + (g + 1) * LANES] = t.astype(BF16)


def _project(x2d, w_in, later_weights, cos_t, sin_t, seq):
    tokens = x2d.shape[0]
    steps = tokens // ROW_TILE
    pos_blocks = seq // ROW_TILE
    gate_blocks = [QKV_WIDTH // D_MODEL + c for c in range((w_in.shape[1] - QKV_WIDTH) // D_MODEL)]
    cast_arrays = [w_in] * len(gate_blocks) + list(later_weights)
    cast_in, cast_out, cast_shapes = [], [], []
    for n, w in enumerate(cast_arrays):
        rows = w.shape[0] // steps
        assert rows * steps == w.shape[0] and rows % BF16_SUBLANES == 0
        width = D_MODEL if n < len(gate_blocks) else w.shape[1]
        col = gate_blocks[n] if n < len(gate_blocks) else 0
        cast_in.append(pl.BlockSpec((rows, width), lambda i, col=col: (i, col)))
        cast_out.append(pl.BlockSpec((rows, width), lambda i: (i, 0)))
        cast_shapes.append(jax.ShapeDtypeStruct((w.shape[0], width), BF16))
    outs = pl.pallas_call(
        _proj_kernel,
        grid=(steps,),
        in_specs=[
            pl.BlockSpec((ROW_TILE, D_MODEL), lambda i: (i, 0)),
            pl.BlockSpec((D_MODEL, QKV_WIDTH), lambda i: (0, 0), pipeline_mode=pl.Buffered(1)),
            pl.BlockSpec((ROW_TILE, LANES), lambda i: (i % pos_blocks, 0)),
            pl.BlockSpec((ROW_TILE, LANES), lambda i: (i % pos_blocks, 0)),
        ] + cast_in,
        out_specs=[
            pl.BlockSpec((ROW_TILE, 3 * SB_WIDTH), lambda i: (i, 0)),
            pl.BlockSpec((ROW_TILE, 3 * DIFF_WIDTH), lambda i: (i, 0)),
        ] + cast_out,
        out_shape=[
            jax.ShapeDtypeStruct((tokens, 3 * SB_WIDTH), BF16),
            jax.ShapeDtypeStruct((tokens, 3 * DIFF_WIDTH), BF16),
        ] + cast_shapes,
        compiler_params=pltpu.CompilerParams(dimension_semantics=("arbitrary",), vmem_limit_bytes=VMEM_LIMIT),
        name="qkv_projection",
    )(x2d, w_in, cos_t, sin_t, *cast_arrays)
    return outs[0], outs[1], outs[2:2 + len(gate_blocks)], outs[2 + len(gate_blocks):]


def _sb_kernel(q_ref, k_ref, v_ref, tri_ref, o_ref, vh_ref):
    n_tiles = q_ref.shape[1] // ATT_TILE
    pairs = range(q_ref.shape[2] // LANES)
    half = ATT_TILE // 2
    tri = tri_ref[...]
    lane = lax.broadcasted_iota(jnp.int32, (ATT_TILE, LANES), 1)
    low = lane < HEAD_DIM
    row = lax.broadcasted_iota(jnp.int32, (half, half), 0)
    col = lax.broadcasted_iota(jnp.int32, (half, half), 1)
    strictly_causal = col < row

    def keys(ref, p, j):
        return ref[0, j * ATT_TILE:(j + 1) * ATT_TILE, p * LANES:(p + 1) * LANES]

    for p in pairs:
        for j in range(n_tiles):
            vt = keys(v_ref, p, j)
            zero = jnp.zeros_like(vt)
            vh_ref[2 * p, j * ATT_TILE:(j + 1) * ATT_TILE, :] = jnp.where(low, vt, zero)
            vh_ref[2 * p + 1, j * ATT_TILE:(j + 1) * ATT_TILE, :] = jnp.where(low, zero, vt)

    def softplus2(z2):
        return jnp.maximum(z2, jnp.log(1.0 + jnp.exp2(jnp.minimum(z2, SOFTPLUS_CLAMP))) * LOG2E)

    tiles = [(p, qi, j) for qi in range(n_tiles) for j in range(qi, -1, -1) for p in pairs]
    qh, scored, summed, acc, carry = {}, {}, {}, {}, {}

    def score(p, qi, j):
        if j == qi:
            q = keys(q_ref, p, qi)
            zero = jnp.zeros_like(q)
            qh[p] = (jnp.where(low, q, zero), jnp.where(low, zero, q))
        kt = keys(k_ref, p, j)
        out = []
        for h in range(2):
            z2 = _dot_nt(qh[p][h], kt)
            if j == qi:
                sp = _quadrants(jnp.where(strictly_causal, softplus2(z2[:half, :half]), 0.0), softplus2(z2[half:, :half]),
                                jnp.where(strictly_causal, softplus2(z2[half:, half:]), 0.0))
            else:
                sp = softplus2(z2)
            out.append((z2, sp))
        scored[p, qi, j] = out

    def in_tile_sums(p, qi, j):
        out = []
        for z2, sp in scored.pop((p, qi, j)):
            hi = lax.bitcast_convert_type(lax.bitcast_convert_type(sp, jnp.uint32) & jnp.uint32(BF16_BITS), F32)
            out.append((z2, _dot(hi, tri) + _dot(sp - hi, tri)))
        summed[p, qi, j] = out

    def value_product(p, qi, j):
        for h, (z2, csum) in enumerate(summed.pop((p, qi, j))):
            x = z2 - csum
            if j == qi:
                a = _quadrants(jnp.where(strictly_causal, jnp.exp2(x[:half, :half]), 0.0), jnp.exp2(x[half:, :half]),
                               jnp.where(strictly_causal, jnp.exp2(x[half:, half:]), 0.0))
            else:
                a = jnp.exp2(x - jnp.concatenate([carry[p, h], carry[p, h]], axis=1))
            if j > 0:
                total = jnp.broadcast_to(csum[:, 0:1], (ATT_TILE, LANES))
                carry[p, h] = total if j == qi else carry[p, h] + total
            t = _dot(a, vh_ref[2 * p + h, j * ATT_TILE:(j + 1) * ATT_TILE, :])
            acc[p] = t if (j == qi and h == 0) else acc[p] + t
        if j == 0:
            o_ref[0, qi * ATT_TILE:(qi + 1) * ATT_TILE, p * LANES:(p + 1) * LANES] = acc.pop(p).astype(BF16)

    skew = len(pairs)
    for s in range(len(tiles) + 2 * skew):
        if s < len(tiles):
            score(*tiles[s])
        if skew <= s < len(tiles) + skew:
            in_tile_sums(*tiles[s - skew])
        if s >= 2 * skew:
            value_product(*tiles[s - 2 * skew])


def _stick_breaking(qkv, tri, batch, seq):
    qkv = qkv.reshape(batch, seq, 3 * SB_WIDTH)
    groups = SB_WIDTH // (SB_PAIRS_PER_STEP * LANES)
    blk = (1, seq, SB_PAIRS_PER_STEP * LANES)
    return pl.pallas_call(
        _sb_kernel,
        grid=(batch, groups),
        in_specs=[
            pl.BlockSpec(blk, lambda b, p: (b, 0, p)),
            pl.BlockSpec(blk, lambda b, p: (b, 0, groups + p)),
            pl.BlockSpec(blk, lambda b, p: (b, 0, 2 * groups + p)),
            _resident((ATT_TILE, ATT_TILE)),
        ],
        out_specs=pl.BlockSpec(blk, lambda b, p: (b, 0, p)),
        out_shape=jax.ShapeDtypeStruct((batch, seq, SB_WIDTH), BF16),
        scratch_shapes=[pltpu.VMEM((2 * SB_PAIRS_PER_STEP, seq, LANES), BF16)],
        compiler_params=pltpu.CompilerParams(dimension_semantics=("arbitrary", "arbitrary"),
                                             vmem_limit_bytes=VMEM_LIMIT),
        name="stick_breaking_attention",
    )(qkv, qkv, qkv, tri)


def _diff_kernel(q_ref, k_ref, v_ref, lq1_ref, lk1_ref, lq2_ref, lk2_ref, gain_ref, o_ref):
    n_tiles = q_ref.shape[1] // ATT_TILE
    half = ATT_TILE // 2
    lam = (jnp.exp(jnp.sum(lq1_ref[...] * lk1_ref[...], keepdims=True))
           - jnp.exp(jnp.sum(lq2_ref[...] * lk2_ref[...], keepdims=True)) + LAMBDA_INIT)
    lane = lax.broadcasted_iota(jnp.int32, (ATT_TILE, LANES), 1)
    low = lane < HEAD_DIM
    row = lax.broadcasted_iota(jnp.int32, (half, half), 0)
    col = lax.broadcasted_iota(jnp.int32, (half, half), 1)
    causal = col <= row
    gain = gain_ref[...]

    def keys(ref, head, j):
        return ref[0, j * ATT_TILE:(j + 1) * ATT_TILE, head * LANES:(head + 1) * LANES]

    def scores(head, qi):
        q = keys(q_ref, head, qi)
        zero = jnp.zeros_like(q)
        return [[_dot_nt(qh, keys(k_ref, head, j)) for j in range(qi + 1)]
                for qh in (jnp.where(low, q, zero), jnp.where(low, zero, q))]

    def weigh(head, qi, zs):
        weights, row_scale = [], []
        for top in (True, False):
            rows = slice(0, half) if top else slice(half, ATT_TILE)
            e, denom = [], []
            for z in zs:
                pieces = [zt[rows, :] for zt in z[:qi]]
                diag = z[qi]
                if top:
                    pieces.append(jnp.where(causal, diag[:half, :half], -jnp.inf))
                else:
                    pieces += [diag[half:, :half], jnp.where(causal, diag[half:, half:], -jnp.inf)]
                m = _row_reduce(jnp.maximum, jnp.max, pieces)
                e.append([jnp.exp2(p - m) for p in pieces])
                denom.append(_row_reduce(jnp.add, jnp.sum, e[-1]))
            rho = lam * denom[0] / denom[1]
            weights.append([e1 - rho * e2 for e1, e2 in zip(*e)])
            row_scale.append(1.0 / denom[0])
        y = None
        for j in range(qi + 1):
            if j < qi:
                w = jnp.concatenate([weights[0][j], weights[1][j]], axis=0)
            else:
                w = _quadrants(weights[0][qi], weights[1][qi], weights[1][qi + 1])
            t = _dot(w, keys(v_ref, head, j))
            y = t if y is None else y + t
        y = y * jnp.concatenate(row_scale, axis=0)
        y = y * lax.rsqrt(jnp.mean(y * y, axis=-1, keepdims=True) + RMS_EPS) * gain
        o_ref[0, qi * ATT_TILE:(qi + 1) * ATT_TILE, head * LANES:(head + 1) * LANES] = (
            y * (1.0 - LAMBDA_INIT)).astype(BF16)

    heads = range(q_ref.shape[2] // LANES)
    pending = [scores(head, 0) for head in heads]
    for qi in range(n_tiles):
        following = [scores(head, qi + 1) if qi + 1 < n_tiles else None for head in heads]
        for head in heads:
            weigh(head, qi, pending[head])
        pending = following


def _differential(qkv, lq1, lk1, lq2, lk2, gain, batch, seq):
    qkv = qkv.reshape(batch, seq, 3 * DIFF_WIDTH)
    blk = (1, seq, DIFF_HEADS_PER_STEP * LANES)
    groups = DIFF_HEADS // DIFF_HEADS_PER_STEP
    return pl.pallas_call(
        _diff_kernel,
        grid=(batch, groups),
        in_specs=[
            pl.BlockSpec(blk, lambda b, h: (b, 0, h)),
            pl.BlockSpec(blk, lambda b, h: (b, 0, groups + h)),
            pl.BlockSpec(blk, lambda b, h: (b, 0, 2 * groups + h)),
            _resident((1, HEAD_DIM)), _resident((1, HEAD_DIM)), _resident((1, HEAD_DIM)), _resident((1, HEAD_DIM)),
            _resident((1, LANES)),
        ],
        out_specs=pl.BlockSpec(blk, lambda b, h: (b, 0, h)),
        out_shape=jax.ShapeDtypeStruct((batch, seq, DIFF_WIDTH), BF16),
        compiler_params=pltpu.CompilerParams(dimension_semantics=("arbitrary", "arbitrary"),
                                             vmem_limit_bytes=VMEM_LIMIT),
        name="differential_attention",
    )(qkv, qkv, qkv, lq1, lk1, lq2, lk2, gain)


def _merge_kernel(x_ref, ya_ref, yb_ref, wga_ref, wgb_ref, wa_ref, wb_ref, wo_ref, gain_ref, bias_ref, o_ref):
    gain, bias = gain_ref[...], bias_ref[...]
    n_chunks = D_MODEL // FF_CHUNK
    units = [(r, c) for r in range(ROW_TILE // SUB_ROWS) for c in range(n_chunks)]
    lhs, pre, acc = {}, {}, {}

    def branches(r, c):
        rows = slice(r * SUB_ROWS, (r + 1) * SUB_ROWS)
        cols = slice(c * FF_CHUNK, (c + 1) * FF_CHUNK)
        if c == 0:
            lhs[r] = (x_ref[rows, :].astype(BF16), ya_ref[rows, :], yb_ref[rows, :])
        xb, ya, yb = lhs[r]
        pre[r, c] = (_dot(xb, wga_ref[:, cols]), _dot(xb, wgb_ref[:, cols]),
                     _dot(ya, wa_ref[:, cols]), _dot(yb, wb_ref[:, cols]))

    def project(r, c):
        gate_a, gate_b, branch_a, branch_b = pre.pop((r, c))
        merged = jax.nn.sigmoid(gate_a) * branch_a + jax.nn.sigmoid(gate_b) * branch_b
        t = _dot(merged, wo_ref[c * FF_CHUNK:(c + 1) * FF_CHUNK, :])
        acc[r] = t if c == 0 else acc[r] + t
        if c == n_chunks - 1:
            rows = slice(r * SUB_ROWS, (r + 1) * SUB_ROWS)
            o_ref[rows, :] = _layer_norm(DEEPNORM_ALPHA * x_ref[rows, :] + acc.pop(r), gain, bias)

    for s in range(len(units) + 1):
        if s < len(units):
            branches(*units[s])
        if s >= 1:
            project(*units[s - 1])


def _merge(x2d, ya, yb, w_gate_a, w_gate_b, w_a, w_b, w_o, gain, bias):
    tokens = x2d.shape[0]
    rows = lambda width: pl.BlockSpec((ROW_TILE, width), lambda i: (i, 0))
    return pl.pallas_call(
        _merge_kernel,
        grid=(tokens // ROW_TILE,),
        in_specs=[
            rows(D_MODEL), rows(SB_WIDTH), rows(DIFF_WIDTH),
            _resident((D_MODEL, D_MODEL)), _resident((D_MODEL, D_MODEL)),
            _resident((SB_WIDTH, D_MODEL)), _resident((DIFF_WIDTH, D_MODEL)),
            _resident((D_MODEL, D_MODEL)), _resident((1, D_MODEL)), _resident((1, D_MODEL)),
        ],
        out_specs=rows(D_MODEL),
        out_shape=jax.ShapeDtypeStruct((tokens, D_MODEL), F32),
        compiler_params=pltpu.CompilerParams(dimension_semantics=("arbitrary",), vmem_limit_bytes=VMEM_LIMIT),
        name="gated_merge_layernorm",
    )(x2d, ya, yb, w_gate_a, w_gate_b, w_a, w_b, w_o, gain, bias)


def _mlp_kernel(x_ref, w1_ref, w2_ref, gain_ref, bias_ref, o_ref):
    gain, bias = gain_ref[...], bias_ref[...]
    n_chunks = D_FF // FF_CHUNK
    units = [(r, c) for r in range(ROW_TILE // SUB_ROWS) for c in range(n_chunks)]
    xb, hidden, acc = {}, {}, {}

    def up(r, c):
        if c == 0:
            xb[r] = x_ref[r * SUB_ROWS:(r + 1) * SUB_ROWS, :].astype(BF16)
        hidden[r, c] = _dot(xb[r], w1_ref[:, c * FF_CHUNK:(c + 1) * FF_CHUNK])

    def down(r, c):
        h = jnp.maximum(hidden.pop((r, c)), 0.0)
        t = _dot(h * h, w2_ref[c * FF_CHUNK:(c + 1) * FF_CHUNK, :])
        acc[r] = t if c == 0 else acc[r] + t
        if c == n_chunks - 1:
            rows = slice(r * SUB_ROWS, (r + 1) * SUB_ROWS)
            o_ref[rows, :] = _layer_norm(DEEPNORM_ALPHA * x_ref[rows, :] + acc.pop(r), gain, bias)

    for s in range(len(units) + 1):
        if s < len(units):
            up(*units[s])
        if s >= 1:
            down(*units[s - 1])


def _mlp(x2d, w1, w2, gain, bias):
    tokens = x2d.shape[0]
    rows = pl.BlockSpec((ROW_TILE, D_MODEL), lambda i: (i, 0))
    return pl.pallas_call(
        _mlp_kernel,
        grid=(tokens // ROW_TILE,),
        in_specs=[rows, _resident((D_MODEL, D_FF)), _resident((D_FF, D_MODEL)),
                  _resident((1, D_MODEL)), _resident((1, D_MODEL))],
        out_specs=rows,
        out_shape=jax.ShapeDtypeStruct((tokens, D_MODEL), F32),
        compiler_params=pltpu.CompilerParams(dimension_semantics=("arbitrary",), vmem_limit_bytes=VMEM_LIMIT),
        name="mlp_layernorm",
    )(x2d, w1, w2, gain, bias)


def _rope_tables(seq):
    inv_freq = ROPE_THETA ** (-jnp.arange(0, HEAD_DIM, 2, dtype=F32) / HEAD_DIM)
    ang = jnp.arange(seq, dtype=F32)[:, None] * inv_freq[None, :]
    cos, sin = jnp.cos(ang), jnp.sin(ang)
    groups = LANES // HEAD_DIM
    return jnp.tile(cos, (1, 2 * groups)), jnp.tile(jnp.concatenate([-sin, sin], axis=1), (1, groups))


def kernel(x, w_in, w_branch_a, w_branch_b, w_out, lambda_q1, lambda_k1, lambda_q2, lambda_k2, subln_gain,
           ln1_gain, ln1_bias, w_ff1, w_ff2, ln2_gain, ln2_bias):
    batch, seq, _ = x.shape
    assert seq % ROW_TILE == 0 and seq % ATT_TILE == 0 and w_in.shape[0] == DEPTH
    cos_t, sin_t = _rope_tables(seq)
    tri = jnp.asarray(np.tril(np.ones((ATT_TILE, ATT_TILE), np.float32)), BF16)
    x2d = x.reshape(batch * seq, D_MODEL)
    for l in range(DEPTH):
        later = (w_branch_a[l], w_branch_b[l], w_out[l], w_ff1[l], w_ff2[l])
        qkv_a, qkv_b, w_gates, (w_a, w_b, w_o, w_1, w_2) = _project(x2d, w_in[l], later, cos_t, sin_t, seq)
        ya = _stick_breaking(qkv_a, tri, batch, seq).reshape(batch * seq, SB_WIDTH)
        yb = _differential(qkv_b, lambda_q1[l][None], lambda_k1[l][None], lambda_q2[l][None], lambda_k2[l][None],
                           subln_gain[l][None], batch, seq).reshape(batch * seq, DIFF_WIDTH)
        x2d = _merge(x2d, ya, yb, *w_gates, w_a, w_b, w_o, ln1_gain[l][None], ln1_bias[l][None])
        x2d = _mlp(x2d, w_1, w_2, ln2_gain[l][None], ln2_bias[l][None])
    return x2d.reshape(batch, seq, D_MODEL)
```

```python
import functools
import math

import numpy as np
import jax
import jax.numpy as jnp
from jax import lax
from jax.experimental import pallas as pl
from jax.experimental.pallas import tpu as pltpu

D_MODEL = 1024
HEAD_DIM = 64
SB_HEADS = 8
DIFF_HEADS = 4
SB_WIDTH = SB_HEADS * HEAD_DIM
DIFF_WIDTH = DIFF_HEADS * 2 * HEAD_DIM
QKV_WIDTH = 3 * SB_WIDTH + 3 * DIFF_WIDTH
D_FF = 4 * D_MODEL
ROPE_THETA = 10000.0
LN_EPS = 1e-5
RMS_EPS = 1e-5
DEPTH = 1
DEEPNORM_ALPHA = (2.0 * DEPTH) ** 0.25
LAMBDA_INIT = 0.8 - 0.6 * math.exp(-0.3 * 0)
LOG2E = math.log2(math.e)
SOFTPLUS_CLAMP = 64.0
QK_SCALE = HEAD_DIM ** -0.5 * LOG2E

LANES = 128
BF16_SUBLANES = 16
BF16_BITS = 0xFFFF0000
ROW_TILE = 1024
SUB_ROWS = 256
ATT_TILE = 256
DIFF_HEADS_PER_STEP = 2
SB_PAIRS_PER_STEP = 2
FF_CHUNK = 512
VMEM_LIMIT = 56 * 1024 * 1024

F32 = jnp.float32
BF16 = jnp.bfloat16


def _dot(a, b):
    return lax.dot_general(a, b, (((1,), (0,)), ((), ())), preferred_element_type=F32)


def _dot_nt(a, b):
    return lax.dot_general(a, b, (((1,), (1,)), ((), ())), preferred_element_type=F32)


def _resident(shape):
    return pl.BlockSpec(shape, lambda *_: (0,) * len(shape), pipeline_mode=pl.Buffered(1))


def _row_reduce(elementwise, lane_reduce, pieces):
    chunks = [p[:, c:c + LANES] for p in pieces for c in range(0, p.shape[1], LANES)]
    return lane_reduce(functools.reduce(elementwise, chunks), axis=1, keepdims=True)


def _quadrants(top_left, bottom_left, bottom_right):
    top = jnp.concatenate([top_left, jnp.zeros_like(top_left)], axis=1)
    return jnp.concatenate([top, jnp.concatenate([bottom_left, bottom_right], axis=1)], axis=0)


def _layer_norm(r, gain, bias):
    mu = jnp.mean(r, axis=-1, keepdims=True)
    c = r - mu
    var = jnp.mean(c * c, axis=-1, keepdims=True)
    return c * lax.rsqrt(var + LN_EPS) * gain + bias


def _proj_kernel(x_ref, w_ref, cos_ref, sin_ref, *refs):
    n_cast = (len(refs) - 2) // 2
    cast_in, (oa_ref, ob_ref), cast_out = refs[:n_cast], refs[n_cast:n_cast + 2], refs[n_cast + 2:]
    for src, dst in zip(cast_in, cast_out):
        dst[...] = src[...].astype(BF16)

    xb = x_ref[...].astype(BF16)
    lane = lax.broadcasted_iota(jnp.int32, (ROW_TILE, LANES), 1)
    first_half = (lane % HEAD_DIM) < (HEAD_DIM // 2)
    pos = pl.multiple_of((pl.program_id(0) % (cos_ref.shape[0] // ROW_TILE)) * ROW_TILE, ROW_TILE)
    cos = cos_ref[pl.ds(pos, ROW_TILE), :]
    sin = sin_ref[pl.ds(pos, ROW_TILE), :]

    def rope(t):
        swapped = jnp.where(first_half, pltpu.roll(t, LANES - HEAD_DIM // 2, 1), pltpu.roll(t, HEAD_DIM // 2, 1))
        return t * cos + swapped * sin

    n_sec = SB_WIDTH // LANES
    for sec in range(QKV_WIDTH // SB_WIDTH):
        acc = _dot(xb, w_ref[:, sec * SB_WIDTH:(sec + 1) * SB_WIDTH].astype(BF16))
        out_ref = oa_ref if sec < 3 else ob_ref
        col0 = (sec % 3) * SB_WIDTH
        for g in range(n_sec):
            t = acc[:, g * LANES:(g + 1) * LANES]
            if sec in (3, 4):
                t = rope(t)
            if sec in (0, 3):
                t = t * QK_SCALE
            out_ref[:, col0 + g * LANES:col0 + (g + 1) * LANES] = t.astype(BF16)


def _project(x2d, w_in, later_weights, cos_t, sin_t, seq):
    tokens = x2d.shape[0]
    steps = tokens // ROW_TILE
    gate_blocks = [QKV_WIDTH // D_MODEL + c for c in range((w_in.shape[1] - QKV_WIDTH) // D_MODEL)]
    cast_arrays = [w_in] * len(gate_blocks) + list(later_weights)
    cast_in, cast_out, cast_shapes = [], [], []
    for n, w in enumerate(cast_arrays):
        rows = w.shape[0] // steps
        assert rows * steps == w.shape[0] and rows % BF16_SUBLANES == 0
        width = D_MODEL if n < len(gate_blocks) else w.shape[1]
        col = gate_blocks[n] if n < len(gate_blocks) else 0
        cast_in.append(pl.BlockSpec((rows, width), lambda i, col=col: (i, col)))
        cast_out.append(pl.BlockSpec((rows, width), lambda i: (i, 0)))
        cast_shapes.append(jax.ShapeDtypeStruct((w.shape[0], width), BF16))
    outs = pl.pallas_call(
        _proj_kernel,
        grid=(steps,),
        in_specs=[
            pl.BlockSpec((ROW_TILE, D_MODEL), lambda i: (i, 0)),
            pl.BlockSpec((D_MODEL, QKV_WIDTH), lambda i: (0, 0), pipeline_mode=pl.Buffered(1)),
            _resident((seq, LANES)), _resident((seq, LANES)),
        ] + cast_in,
        out_specs=[
            pl.BlockSpec((ROW_TILE, 3 * SB_WIDTH), lambda i: (i, 0)),
            pl.BlockSpec((ROW_TILE, 3 * DIFF_WIDTH), lambda i: (i, 0)),
        ] + cast_out,
        out_shape=[
            jax.ShapeDtypeStruct((tokens, 3 * SB_WIDTH), BF16),
            jax.ShapeDtypeStruct((tokens, 3 * DIFF_WIDTH), BF16),
        ] + cast_shapes,
        compiler_params=pltpu.CompilerParams(dimension_semantics=("arbitrary",), vmem_limit_bytes=VMEM_LIMIT),
        name="qkv_projection",
    )(x2d, w_in, cos_t, sin_t, *cast_arrays)
    return outs[0], outs[1], outs[2:2 + len(gate_blocks)], outs[2 + len(gate_blocks):]


def _sb_kernel(q_ref, k_ref, v_ref, tri_ref, o_ref, vh_ref):
    n_tiles = q_ref.shape[1] // ATT_TILE
    pairs = range(q_ref.shape[2] // LANES)
    half = ATT_TILE // 2
    tri = tri_ref[...]
    lane = lax.broadcasted_iota(jnp.int32, (ATT_TILE, LANES), 1)
    low = lane < HEAD_DIM
    row = lax.broadcasted_iota(jnp.int32, (half, half), 0)
    col = lax.broadcasted_iota(jnp.int32, (half, half), 1)
    strictly_causal = col < row

    def keys(ref, p, j):
        return ref[0, j * ATT_TILE:(j + 1) * ATT_TILE, p * LANES:(p + 1) * LANES]

    for p in pairs:
        for j in range(n_tiles):
            vt = keys(v_ref, p, j)
            zero = jnp.zeros_like(vt)
            vh_ref[2 * p, j * ATT_TILE:(j + 1) * ATT_TILE, :] = jnp.where(low, vt, zero)
            vh_ref[2 * p + 1, j * ATT_TILE:(j + 1) * ATT_TILE, :] = jnp.where(low, zero, vt)

    def softplus2(z2):
        return jnp.maximum(z2, jnp.log(1.0 + jnp.exp2(jnp.minimum(z2, SOFTPLUS_CLAMP))) * LOG2E)

    tiles = [(p, qi, j) for p in pairs for qi in range(n_tiles) for j in range(qi, -1, -1)]
    qh, scored, summed, acc, carry = {}, {}, {}, {}, {}

    def score(p, qi, j):
        if j == qi:
            q = keys(q_ref, p, qi)
            zero = jnp.zeros_like(q)
            qh[p] = (jnp.where(low, q, zero), jnp.where(low, zero, q))
        kt = keys(k_ref, p, j)
        out = []
        for h in range(2):
            z2 = _dot_nt(qh[p][h], kt)
            if j == qi:
                sp = _quadrants(jnp.where(strictly_causal, softplus2(z2[:half, :half]), 0.0), softplus2(z2[half:, :half]),
                                jnp.where(strictly_causal, softplus2(z2[half:, half:]), 0.0))
            else:
                sp = softplus2(z2)
            out.append((z2, sp))
        scored[p, qi, j] = out

    def in_tile_sums(p, qi, j):
        out = []
        for z2, sp in scored.pop((p, qi, j)):
            hi = lax.bitcast_convert_type(lax.bitcast_convert_type(sp, jnp.uint32) & jnp.uint32(BF16_BITS), F32)
            out.append((z2, _dot(hi, tri) + _dot(sp - hi, tri)))
        summed[p, qi, j] = out

    def value_product(p, qi, j):
        for h, (z2, csum) in enumerate(summed.pop((p, qi, j))):
            x = z2 - csum
            if j == qi:
                a = _quadrants(jnp.where(strictly_causal, jnp.exp2(x[:half, :half]), 0.0), jnp.exp2(x[half:, :half]),
                               jnp.where(strictly_causal, jnp.exp2(x[half:, half:]), 0.0))
            else:
                a = jnp.exp2(x - jnp.concatenate([carry[p, h], carry[p, h]], axis=1))
            if j > 0:
                total = jnp.broadcast_to(csum[:, 0:1], (ATT_TILE, LANES))
                carry[p, h] = total if j == qi else carry[p, h] + total
            t = _dot(a, vh_ref[2 * p + h, j * ATT_TILE:(j + 1) * ATT_TILE, :])
            acc[p] = t if (j == qi and h == 0) else acc[p] + t
        if j == 0:
            o_ref[0, qi * ATT_TILE:(qi + 1) * ATT_TILE, p * LANES:(p + 1) * LANES] = acc.pop(p).astype(BF16)

    for s in range(len(tiles) + 2):
        if s < len(tiles):
            score(*tiles[s])
        if 1 <= s <= len(tiles):
            in_tile_sums(*tiles[s - 1])
        if s >= 2:
            value_product(*tiles[s - 2])


def _stick_breaking(qkv, tri, batch, seq):
    qkv = qkv.reshape(batch, seq, 3 * SB_WIDTH)
    groups = SB_WIDTH // (SB_PAIRS_PER_STEP * LANES)
    blk = (1, seq, SB_PAIRS_PER_STEP * LANES)
    return pl.pallas_call(
        _sb_kernel,
        grid=(batch, groups),
        in_specs=[
            pl.BlockSpec(blk, lambda b, p: (b, 0, p)),
            pl.BlockSpec(blk, lambda b, p: (b, 0, groups + p)),
            pl.BlockSpec(blk, lambda b, p: (b, 0, 2 * groups + p)),
            _resident((ATT_TILE, ATT_TILE)),
        ],
        out_specs=pl.BlockSpec(blk, lambda b, p: (b, 0, p)),
        out_shape=jax.ShapeDtypeStruct((batch, seq, SB_WIDTH), BF16),
        scratch_shapes=[pltpu.VMEM((2 * SB_PAIRS_PER_STEP, seq, LANES), BF16)],
        compiler_params=pltpu.CompilerParams(dimension_semantics=("arbitrary", "arbitrary"),
                                             vmem_limit_bytes=VMEM_LIMIT),
        name="stick_breaking_attention",
    )(qkv, qkv, qkv, tri)


def _diff_kernel(q_ref, k_ref, v_ref, lq1_ref, lk1_ref, lq2_ref, lk2_ref, gain_ref, o_ref):
    n_tiles = q_ref.shape[1] // ATT_TILE
    half = ATT_TILE // 2
    lam = (jnp.exp(jnp.sum(lq1_ref[...] * lk1_ref[...], keepdims=True))
           - jnp.exp(jnp.sum(lq2_ref[...] * lk2_ref[...], keepdims=True)) + LAMBDA_INIT)
    lane = lax.broadcasted_iota(jnp.int32, (ATT_TILE, LANES), 1)
    low = lane < HEAD_DIM
    row = lax.broadcasted_iota(jnp.int32, (half, half), 0)
    col = lax.broadcasted_iota(jnp.int32, (half, half), 1)
    causal = col <= row
    gain = gain_ref[...]

    def keys(ref, head, j):
        return ref[0, j * ATT_TILE:(j + 1) * ATT_TILE, head * LANES:(head + 1) * LANES]

    def scores(head, qi):
        q = keys(q_ref, head, qi)
        zero = jnp.zeros_like(q)
        return [[_dot_nt(qh, keys(k_ref, head, j)) for j in range(qi + 1)]
                for qh in (jnp.where(low, q, zero), jnp.where(low, zero, q))]

    def weigh(head, qi, zs):
        weights, row_scale = [], []
        for top in (True, False):
            rows = slice(0, half) if top else slice(half, ATT_TILE)
            e, denom = [], []
            for z in zs:
                pieces = [zt[rows, :] for zt in z[:qi]]
                diag = z[qi]
                if top:
                    pieces.append(jnp.where(causal, diag[:half, :half], -jnp.inf))
                else:
                    pieces += [diag[half:, :half], jnp.where(causal, diag[half:, half:], -jnp.inf)]
                m = _row_reduce(jnp.maximum, jnp.max, pieces)
                e.append([jnp.exp2(p - m) for p in pieces])
                denom.append(_row_reduce(jnp.add, jnp.sum, e[-1]))
            rho = lam * denom[0] / denom[1]
            weights.append([e1 - rho * e2 for e1, e2 in zip(*e)])
            row_scale.append(1.0 / denom[0])
        y = None
        for j in range(qi + 1):
            if j < qi:
                w = jnp.concatenate([weights[0][j], weights[1][j]], axis=0)
            else:
                w = _quadrants(weights[0][qi], weights[1][qi], weights[1][qi + 1])
            t = _dot(w, keys(v_ref, head, j))
            y = t if y is None else y + t
        y = y * jnp.concatenate(row_scale, axis=0)
        y = y * lax.rsqrt(jnp.mean(y * y, axis=-1, keepdims=True) + RMS_EPS) * gain
        o_ref[0, qi * ATT_TILE:(qi + 1) * ATT_TILE, head * LANES:(head + 1) * LANES] = (
            y * (1.0 - LAMBDA_INIT)).astype(BF16)

    heads = range(q_ref.shape[2] // LANES)
    pending = [scores(head, 0) for head in heads]
    for qi in range(n_tiles):
        following = [scores(head, qi + 1) if qi + 1 < n_tiles else None for head in heads]
        for head in heads:
            weigh(head, qi, pending[head])
        pending = following


def _differential(qkv, lq1, lk1, lq2, lk2, gain, batch, seq):
    qkv = qkv.reshape(batch, seq, 3 * DIFF_WIDTH)
    blk = (1, seq, DIFF_HEADS_PER_STEP * LANES)
    groups = DIFF_HEADS // DIFF_HEADS_PER_STEP
    return pl.pallas_call(
        _diff_kernel,
        grid=(batch, groups),
        in_specs=[
            pl.BlockSpec(blk, lambda b, h: (b, 0, h)),
            pl.BlockSpec(blk, lambda b, h: (b, 0, groups + h)),
            pl.BlockSpec(blk, lambda b, h: (b, 0, 2 * groups + h)),
            _resident((1, HEAD_DIM)), _resident((1, HEAD_DIM)), _resident((1, HEAD_DIM)), _resident((1, HEAD_DIM)),
            _resident((1, LANES)),
        ],
        out_specs=pl.BlockSpec(blk, lambda b, h: (b, 0, h)),
        out_shape=jax.ShapeDtypeStruct((batch, seq, DIFF_WIDTH), BF16),
        compiler_params=pltpu.CompilerParams(dimension_semantics=("arbitrary", "arbitrary"),
                                             vmem_limit_bytes=VMEM_LIMIT),
        name="differential_attention",
    )(qkv, qkv, qkv, lq1, lk1, lq2, lk2, gain)


def _merge_kernel(x_ref, ya_ref, yb_ref, wga_ref, wgb_ref, wa_ref, wb_ref, wo_ref, gain_ref, bias_ref, o_ref):
    gain, bias = gain_ref[...], bias_ref[...]
    n_chunks = D_MODEL // FF_CHUNK
    units = [(r, c) for r in range(ROW_TILE // SUB_ROWS) for c in range(n_chunks)]
    lhs, pre, acc = {}, {}, {}

    def branches(r, c):
        rows = slice(r * SUB_ROWS, (r + 1) * SUB_ROWS)
        cols = slice(c * FF_CHUNK, (c + 1) * FF_CHUNK)
        if c == 0:
            lhs[r] = (x_ref[rows, :].astype(BF16), ya_ref[rows, :], yb_ref[rows, :])
        xb, ya, yb = lhs[r]
        pre[r, c] = (_dot(xb, wga_ref[:, cols]), _dot(xb, wgb_ref[:, cols]),
                     _dot(ya, wa_ref[:, cols]), _dot(yb, wb_ref[:, cols]))

    def project(r, c):
        gate_a, gate_b, branch_a, branch_b = pre.pop((r, c))
        merged = jax.nn.sigmoid(gate_a) * branch_a + jax.nn.sigmoid(gate_b) * branch_b
        t = _dot(merged, wo_ref[c * FF_CHUNK:(c + 1) * FF_CHUNK, :])
        acc[r] = t if c == 0 else acc[r] + t
        if c == n_chunks - 1:
            rows = slice(r * SUB_ROWS, (r + 1) * SUB_ROWS)
            o_ref[rows, :] = _layer_norm(DEEPNORM_ALPHA * x_ref[rows, :] + acc.pop(r), gain, bias)

    for s in range(len(units) + 1):
        if s < len(units):
            branches(*units[s])
        if s >= 1:
            project(*units[s - 1])


def _merge(x2d, ya, yb, w_gate_a, w_gate_b, w_a, w_b, w_o, gain, bias):
    tokens = x2d.shape[0]
    rows = lambda width: pl.BlockSpec((ROW_TILE, width), lambda i: (i, 0))
    return pl.pallas_call(
        _merge_kernel,
        grid=(tokens // ROW_TILE,),
        in_specs=[
            rows(D_MODEL), rows(SB_WIDTH), rows(DIFF_WIDTH),
            _resident((D_MODEL, D_MODEL)), _resident((D_MODEL, D_MODEL)),
            _resident((SB_WIDTH, D_MODEL)), _resident((DIFF_WIDTH, D_MODEL)),
            _resident((D_MODEL, D_MODEL)), _resident((1, D_MODEL)), _resident((1, D_MODEL)),
        ],
        out_specs=rows(D_MODEL),
        out_shape=jax.ShapeDtypeStruct((tokens, D_MODEL), F32),
        compiler_params=pltpu.CompilerParams(dimension_semantics=("arbitrary",), vmem_limit_bytes=VMEM_LIMIT),
        name="gated_merge_layernorm",
    )(x2d, ya, yb, w_gate_a, w_gate_b, w_a, w_b, w_o, gain, bias)


def _mlp_kernel(x_ref, w1_ref, w2_ref, gain_ref, bias_ref, o_ref):
    gain, bias = gain_ref[...], bias_ref[...]
    n_chunks = D_FF // FF_CHUNK
    units = [(r, c) for r in range(ROW_TILE // SUB_ROWS) for c in range(n_chunks)]
    xb, hidden, acc = {}, {}, {}

    def up(r, c):
        if c == 0:
            xb[r] = x_ref[r * SUB_ROWS:(r + 1) * SUB_ROWS, :].astype(BF16)
        hidden[r, c] = _dot(xb[r], w1_ref[:, c * FF_CHUNK:(c + 1) * FF_CHUNK])

    def down(r, c):
        h = jnp.maximum(hidden.pop((r, c)), 0.0)
        t = _dot(h * h, w2_ref[c * FF_CHUNK:(c + 1) * FF_CHUNK, :])
        acc[r] = t if c == 0 else acc[r] + t
        if c == n_chunks - 1:
            rows = slice(r * SUB_ROWS, (r + 1) * SUB_ROWS)
            o_ref[rows, :] = _layer_norm(DEEPNORM_ALPHA * x_ref[rows, :] + acc.pop(r), gain, bias)

    for s in range(len(units) + 1):
        if s < len(units):
            up(*units[s])
        if s >= 1:
            down(*units[s - 1])


def _mlp(x2d, w1, w2, gain, bias):
    tokens = x2d.shape[0]
    rows = pl.BlockSpec((ROW_TILE, D_MODEL), lambda i: (i, 0))
    return pl.pallas_call(
        _mlp_kernel,
        grid=(tokens // ROW_TILE,),
        in_specs=[rows, _resident((D_MODEL, D_FF)), _resident((D_FF, D_MODEL)),
                  _resident((1, D_MODEL)), _resident((1, D_MODEL))],
        out_specs=rows,
        out_shape=jax.ShapeDtypeStruct((tokens, D_MODEL), F32),
        compiler_params=pltpu.CompilerParams(dimension_semantics=("arbitrary",), vmem_limit_bytes=VMEM_LIMIT),
        name="mlp_layernorm",
    )(x2d, w1, w2, gain, bias)


def _rope_tables(seq):
    inv_freq = ROPE_THETA ** (-jnp.arange(0, HEAD_DIM, 2, dtype=F32) / HEAD_DIM)
    ang = jnp.arange(seq, dtype=F32)[:, None] * inv_freq[None, :]
    cos, sin = jnp.cos(ang), jnp.sin(ang)
    groups = LANES // HEAD_DIM
    return jnp.tile(cos, (1, 2 * groups)), jnp.tile(jnp.concatenate([-sin, sin], axis=1), (1, groups))


def kernel(x, w_in, w_branch_a, w_branch_b, w_out, lambda_q1, lambda_k1, lambda_q2, lambda_k2, subln_gain,
           ln1_gain, ln1_bias, w_ff1, w_ff2, ln2_gain, ln2_bias):
    batch, seq, _ = x.shape
    assert seq % ROW_TILE == 0 and seq % ATT_TILE == 0 and w_in.shape[0] == DEPTH
    cos_t, sin_t = _rope_tables(seq)
    tri = jnp.asarray(np.tril(np.ones((ATT_TILE, ATT_TILE), np.float32)), BF16)
    x2d = x.reshape(batch * seq, D_MODEL)
    for l in range(DEPTH):
        later = (w_branch_a[l], w_branch_b[l], w_out[l], w_ff1[l], w_ff2[l])
        qkv_a, qkv_b, w_gates, (w_a, w_b, w_o, w_1, w_2) = _project(x2d, w_in[l], later, cos_t, sin_t, seq)
        ya = _stick_breaking(qkv_a, tri, batch, seq).reshape(batch * seq, SB_WIDTH)
        yb = _differential(qkv_b, lambda_q1[l][None], lambda_k1[l][None], lambda_q2[l][None], lambda_k2[l][None],
                           subln_gain[l][None], batch, seq).reshape(batch * seq, DIFF_WIDTH)
        x2d = _merge(x2d, ya, yb, *w_gates, w_a, w_b, w_o, ln1_gain[l][None], ln1_bias[l][None])
        x2d = _mlp(x2d, w_1, w_2, ln2_gain[l][None], ln2_bias[l][None])
    return x2d.reshape(batch, seq, D_MODEL)
```

```python
import functools
import math

import numpy as np
import jax
import jax.numpy as jnp
from jax import lax
from jax.experimental import pallas as pl
from jax.experimental.pallas import tpu as pltpu

D_MODEL = 1024
HEAD_DIM = 64
SB_HEADS = 8
DIFF_HEADS = 4
SB_WIDTH = SB_HEADS * HEAD_DIM
DIFF_WIDTH = DIFF_HEADS * 2 * HEAD_DIM
QKV_WIDTH = 3 * SB_WIDTH + 3 * DIFF_WIDTH
D_FF = 4 * D_MODEL
ROPE_THETA = 10000.0
LN_EPS = 1e-5
RMS_EPS = 1e-5
DEPTH = 1
DEEPNORM_ALPHA = (2.0 * DEPTH) ** 0.25
LAMBDA_INIT = 0.8 - 0.6 * math.exp(-0.3 * 0)
LOG2E = math.log2(math.e)
SOFTPLUS_CLAMP = 64.0
QK_SCALE = HEAD_DIM ** -0.5 * LOG2E

LANES = 128
BF16_SUBLANES = 16
BF16_BITS = 0xFFFF0000
ROW_TILE = 1024
SUB_ROWS = 256
ATT_TILE = 256
DIFF_HEADS_PER_STEP = 2
SB_PAIRS_PER_STEP = 2
FF_CHUNK = 512
VMEM_LIMIT = 56 * 1024 * 1024

F32 = jnp.float32
BF16 = jnp.bfloat16


def _dot(a, b):
    return lax.dot_general(a, b, (((1,), (0,)), ((), ())), preferred_element_type=F32)


def _dot_nt(a, b):
    return lax.dot_general(a, b, (((1,), (1,)), ((), ())), preferred_element_type=F32)


def _resident(shape):
    return pl.BlockSpec(shape, lambda *_: (0,) * len(shape), pipeline_mode=pl.Buffered(1))


def _row_reduce(elementwise, lane_reduce, pieces):
    chunks = [p[:, c:c + LANES] for p in pieces for c in range(0, p.shape[1], LANES)]
    return lane_reduce(functools.reduce(elementwise, chunks), axis=1, keepdims=True)


def _quadrants(top_left, bottom_left, bottom_right):
    top = jnp.concatenate([top_left, jnp.zeros_like(top_left)], axis=1)
    return jnp.concatenate([top, jnp.concatenate([bottom_left, bottom_right], axis=1)], axis=0)


def _layer_norm(r, gain, bias):
    mu = jnp.mean(r, axis=-1, keepdims=True)
    c = r - mu
    var = jnp.mean(c * c, axis=-1, keepdims=True)
    return c * lax.rsqrt(var + LN_EPS) * gain + bias


def _proj_kernel(x_ref, w_ref, cos_ref, sin_ref, *refs):
    n_cast = (len(refs) - 2) // 2
    cast_in, (oa_ref, ob_ref), cast_out = refs[:n_cast], refs[n_cast:n_cast + 2], refs[n_cast + 2:]
    for src, dst in zip(cast_in, cast_out):
        dst[...] = src[...].astype(BF16)

    xb = x_ref[...].astype(BF16)
    lane = lax.broadcasted_iota(jnp.int32, (ROW_TILE, LANES), 1)
    first_half = (lane % HEAD_DIM) < (HEAD_DIM // 2)
    pos = pl.multiple_of((pl.program_id(0) % (cos_ref.shape[0] // ROW_TILE)) * ROW_TILE, ROW_TILE)
    cos = cos_ref[pl.ds(pos, ROW_TILE), :]
    sin = sin_ref[pl.ds(pos, ROW_TILE), :]

    def rope(t):
        swapped = jnp.where(first_half, pltpu.roll(t, LANES - HEAD_DIM // 2, 1), pltpu.roll(t, HEAD_DIM // 2, 1))
        return t * cos + swapped * sin

    n_sec = SB_WIDTH // LANES
    for sec in range(QKV_WIDTH // SB_WIDTH):
        acc = _dot(xb, w_ref[:, sec * SB_WIDTH:(sec + 1) * SB_WIDTH].astype(BF16))
        out_ref = oa_ref if sec < 3 else ob_ref
        col0 = (sec % 3) * SB_WIDTH
        for g in range(n_sec):
            t = acc[:, g * LANES:(g + 1) * LANES]
            if sec in (3, 4):
                t = rope(t)
            if sec in (0, 3):
                t = t * QK_SCALE
            out_ref[:, col0 + g * LANES:col0 + (g + 1) * LANES] = t.astype(BF16)


def _project(x2d, w_in, later_weights, cos_t, sin_t, seq):
    tokens = x2d.shape[0]
    steps = tokens // ROW_TILE
    gate_blocks = [QKV_WIDTH // D_MODEL + c for c in range((w_in.shape[1] - QKV_WIDTH) // D_MODEL)]
    cast_arrays = [w_in] * len(gate_blocks) + list(later_weights)
    cast_in, cast_out, cast_shapes = [], [], []
    for n, w in enumerate(cast_arrays):
        rows = w.shape[0] // steps
        assert rows * steps == w.shape[0] and rows % BF16_SUBLANES == 0
        width = D_MODEL if n < len(gate_blocks) else w.shape[1]
        col = gate_blocks[n] if n < len(gate_blocks) else 0
        cast_in.append(pl.BlockSpec((rows, width), lambda i, col=col: (i, col)))
        cast_out.append(pl.BlockSpec((rows, width), lambda i: (i, 0)))
        cast_shapes.append(jax.ShapeDtypeStruct((w.shape[0], width), BF16))
    outs = pl.pallas_call(
        _proj_kernel,
        grid=(steps,),
        in_specs=[
            pl.BlockSpec((ROW_TILE, D_MODEL), lambda i: (i, 0)),
            pl.BlockSpec((D_MODEL, QKV_WIDTH), lambda i: (0, 0), pipeline_mode=pl.Buffered(1)),
            _resident((seq, LANES)), _resident((seq, LANES)),
        ] + cast_in,
        out_specs=[
            pl.BlockSpec((ROW_TILE, 3 * SB_WIDTH), lambda i: (i, 0)),
            pl.BlockSpec((ROW_TILE, 3 * DIFF_WIDTH), lambda i: (i, 0)),
        ] + cast_out,
        out_shape=[
            jax.ShapeDtypeStruct((tokens, 3 * SB_WIDTH), BF16),
            jax.ShapeDtypeStruct((tokens, 3 * DIFF_WIDTH), BF16),
        ] + cast_shapes,
        compiler_params=pltpu.CompilerParams(dimension_semantics=("arbitrary",), vmem_limit_bytes=VMEM_LIMIT),
        name="qkv_projection",
    )(x2d, w_in, cos_t, sin_t, *cast_arrays)
    return outs[0], outs[1], outs[2:2 + len(gate_blocks)], outs[2 + len(gate_blocks):]


def _sb_kernel(q_ref, k_ref, v_ref, tri_ref, o_ref, vh_ref):
    n_tiles = q_ref.shape[1] // ATT_TILE
    pairs = range(q_ref.shape[2] // LANES)
    half = ATT_TILE // 2
    tri = tri_ref[...]
    lane = lax.broadcasted_iota(jnp.int32, (ATT_TILE, LANES), 1)
    low = lane < HEAD_DIM
    row = lax.broadcasted_iota(jnp.int32, (half, half), 0)
    col = lax.broadcasted_iota(jnp.int32, (half, half), 1)
    strictly_causal = col < row

    def keys(ref, p, j):
        return ref[0, j * ATT_TILE:(j + 1) * ATT_TILE, p * LANES:(p + 1) * LANES]

    for p in pairs:
        for j in range(n_tiles):
            vt = keys(v_ref, p, j)
            zero = jnp.zeros_like(vt)
            vh_ref[2 * p, j * ATT_TILE:(j + 1) * ATT_TILE, :] = jnp.where(low, vt, zero)
            vh_ref[2 * p + 1, j * ATT_TILE:(j + 1) * ATT_TILE, :] = jnp.where(low, zero, vt)

    def softplus2(z2):
        return jnp.maximum(z2, jnp.log(1.0 + jnp.exp2(jnp.minimum(z2, SOFTPLUS_CLAMP))) * LOG2E)

    tiles = [(p, qi, j) for p in pairs for qi in range(n_tiles) for j in range(qi, -1, -1)]
    qh, scored, summed, acc, carry = {}, {}, {}, {}, {}

    def score(p, qi, j):
        if j == qi:
            q = keys(q_ref, p, qi)
            zero = jnp.zeros_like(q)
            qh[p] = (jnp.where(low, q, zero), jnp.where(low, zero, q))
        kt = keys(k_ref, p, j)
        out = []
        for h in range(2):
            z2 = _dot_nt(qh[p][h], kt)
            if j == qi:
                sp = _quadrants(jnp.where(strictly_causal, softplus2(z2[:half, :half]), 0.0), softplus2(z2[half:, :half]),
                                jnp.where(strictly_causal, softplus2(z2[half:, half:]), 0.0))
            else:
                sp = softplus2(z2)
            out.append((z2, sp))
        scored[p, qi, j] = out

    def in_tile_sums(p, qi, j):
        items = scored.pop((p, qi, j))
        his = [lax.bitcast_convert_type(lax.bitcast_convert_type(sp, jnp.uint32) & jnp.uint32(BF16_BITS), F32)
               for _, sp in items]
        high = [_dot(hi, tri) for hi in his]
        rest = [_dot(sp - hi, tri) for (_, sp), hi in zip(items, his)]
        summed[p, qi, j] = [(z2, a + b) for (z2, _), a, b in zip(items, high, rest)]

    def value_product(p, qi, j):
        for h, (z2, csum) in enumerate(summed.pop((p, qi, j))):
            x = z2 - csum
            if j == qi:
                a = _quadrants(jnp.where(strictly_causal, jnp.exp2(x[:half, :half]), 0.0), jnp.exp2(x[half:, :half]),
                               jnp.where(strictly_causal, jnp.exp2(x[half:, half:]), 0.0))
            else:
                a = jnp.exp2(x - jnp.concatenate([carry[p, h], carry[p, h]], axis=1))
            if j > 0:
                total = jnp.broadcast_to(csum[:, 0:1], (ATT_TILE, LANES))
                carry[p, h] = total if j == qi else carry[p, h] + total
            t = _dot(a, vh_ref[2 * p + h, j * ATT_TILE:(j + 1) * ATT_TILE, :])
            acc[p] = t if (j == qi and h == 0) else acc[p] + t
        if j == 0:
            o_ref[0, qi * ATT_TILE:(qi + 1) * ATT_TILE, p * LANES:(p + 1) * LANES] = acc.pop(p).astype(BF16)

    for s in range(len(tiles) + 2):
        if s < len(tiles):
            score(*tiles[s])
        if 1 <= s <= len(tiles):
            in_tile_sums(*tiles[s - 1])
        if s >= 2:
            value_product(*tiles[s - 2])


def _stick_breaking(qkv, tri, batch, seq):
    qkv = qkv.reshape(batch, seq, 3 * SB_WIDTH)
    groups = SB_WIDTH // (SB_PAIRS_PER_STEP * LANES)
    blk = (1, seq, SB_PAIRS_PER_STEP * LANES)
    return pl.pallas_call(
        _sb_kernel,
        grid=(batch, groups),
        in_specs=[
            pl.BlockSpec(blk, lambda b, p: (b, 0, p)),
            pl.BlockSpec(blk, lambda b, p: (b, 0, groups + p)),
            pl.BlockSpec(blk, lambda b, p: (b, 0, 2 * groups + p)),
            _resident((ATT_TILE, ATT_TILE)),
        ],
        out_specs=pl.BlockSpec(blk, lambda b, p: (b, 0, p)),
        out_shape=jax.ShapeDtypeStruct((batch, seq, SB_WIDTH), BF16),
        scratch_shapes=[pltpu.VMEM((2 * SB_PAIRS_PER_STEP, seq, LANES), BF16)],
        compiler_params=pltpu.CompilerParams(dimension_semantics=("arbitrary", "arbitrary"),
                                             vmem_limit_bytes=VMEM_LIMIT),
        name="stick_breaking_attention",
    )(qkv, qkv, qkv, tri)


def _diff_kernel(q_ref, k_ref, v_ref, lq1_ref, lk1_ref, lq2_ref, lk2_ref, gain_ref, o_ref):
    n_tiles = q_ref.shape[1] // ATT_TILE
    half = ATT_TILE // 2
    lam = (jnp.exp(jnp.sum(lq1_ref[...] * lk1_ref[...], keepdims=True))
           - jnp.exp(jnp.sum(lq2_ref[...] * lk2_ref[...], keepdims=True)) + LAMBDA_INIT)
    lane = lax.broadcasted_iota(jnp.int32, (ATT_TILE, LANES), 1)
    low = lane < HEAD_DIM
    row = lax.broadcasted_iota(jnp.int32, (half, half), 0)
    col = lax.broadcasted_iota(jnp.int32, (half, half), 1)
    causal = col <= row
    gain = gain_ref[...]

    def keys(ref, head, j):
        return ref[0, j * ATT_TILE:(j + 1) * ATT_TILE, head * LANES:(head + 1) * LANES]

    def scores(head, qi):
        q = keys(q_ref, head, qi)
        zero = jnp.zeros_like(q)
        return [[_dot_nt(qh, keys(k_ref, head, j)) for j in range(qi + 1)]
                for qh in (jnp.where(low, q, zero), jnp.where(low, zero, q))]

    def weigh(head, qi, zs):
        weights, row_scale = [], []
        for top in (True, False):
            rows = slice(0, half) if top else slice(half, ATT_TILE)
            e, denom = [], []
            for z in zs:
                pieces = [zt[rows, :] for zt in z[:qi]]
                diag = z[qi]
                if top:
                    pieces.append(jnp.where(causal, diag[:half, :half], -jnp.inf))
                else:
                    pieces += [diag[half:, :half], jnp.where(causal, diag[half:, half:], -jnp.inf)]
                m = _row_reduce(jnp.maximum, jnp.max, pieces)
                e.append([jnp.exp2(p - m) for p in pieces])
                denom.append(_row_reduce(jnp.add, jnp.sum, e[-1]))
            rho = lam * denom[0] / denom[1]
            weights.append([e1 - rho * e2 for e1, e2 in zip(*e)])
            row_scale.append(1.0 / denom[0])
        y = None
        for j in range(qi + 1):
            if j < qi:
                w = jnp.concatenate([weights[0][j], weights[1][j]], axis=0)
            else:
                w = _quadrants(weights[0][qi], weights[1][qi], weights[1][qi + 1])
            t = _dot(w, keys(v_ref, head, j))
            y = t if y is None else y + t
        y = y * jnp.concatenate(row_scale, axis=0)
        y = y * lax.rsqrt(jnp.mean(y * y, axis=-1, keepdims=True) + RMS_EPS) * gain
        o_ref[0, qi * ATT_TILE:(qi + 1) * ATT_TILE, head * LANES:(head + 1) * LANES] = (
            y * (1.0 - LAMBDA_INIT)).astype(BF16)

    heads = range(q_ref.shape[2] // LANES)
    pending = [scores(head, 0) for head in heads]
    for qi in range(n_tiles):
        following = [scores(head, qi + 1) if qi + 1 < n_tiles else None for head in heads]
        for head in heads:
            weigh(head, qi, pending[head])
        pending = following


def _differential(qkv, lq1, lk1, lq2, lk2, gain, batch, seq):
    qkv = qkv.reshape(batch, seq, 3 * DIFF_WIDTH)
    blk = (1, seq, DIFF_HEADS_PER_STEP * LANES)
    groups = DIFF_HEADS // DIFF_HEADS_PER_STEP
    return pl.pallas_call(
        _diff_kernel,
        grid=(batch, groups),
        in_specs=[
            pl.BlockSpec(blk, lambda b, h: (b, 0, h)),
            pl.BlockSpec(blk, lambda b, h: (b, 0, groups + h)),
            pl.BlockSpec(blk, lambda b, h: (b, 0, 2 * groups + h)),
            _resident((1, HEAD_DIM)), _resident((1, HEAD_DIM)), _resident((1, HEAD_DIM)), _resident((1, HEAD_DIM)),
            _resident((1, LANES)),
        ],
        out_specs=pl.BlockSpec(blk, lambda b, h: (b, 0, h)),
        out_shape=jax.ShapeDtypeStruct((batch, seq, DIFF_WIDTH), BF16),
        compiler_params=pltpu.CompilerParams(dimension_semantics=("arbitrary", "arbitrary"),
                                             vmem_limit_bytes=VMEM_LIMIT),
        name="differential_attention",
    )(qkv, qkv, qkv, lq1, lk1, lq2, lk2, gain)


def _merge_kernel(x_ref, ya_ref, yb_ref, wga_ref, wgb_ref, wa_ref, wb_ref, wo_ref, gain_ref, bias_ref, o_ref):
    gain, bias = gain_ref[...], bias_ref[...]
    n_chunks = D_MODEL // FF_CHUNK
    units = [(r, c) for r in range(ROW_TILE // SUB_ROWS) for c in range(n_chunks)]
    lhs, pre, acc = {}, {}, {}

    def branches(r, c):
        rows = slice(r * SUB_ROWS, (r + 1) * SUB_ROWS)
        cols = slice(c * FF_CHUNK, (c + 1) * FF_CHUNK)
        if c == 0:
            lhs[r] = (x_ref[rows, :].astype(BF16), ya_ref[rows, :], yb_ref[rows, :])
        xb, ya, yb = lhs[r]
        pre[r, c] = (_dot(xb, wga_ref[:, cols]), _dot(xb, wgb_ref[:, cols]),
                     _dot(ya, wa_ref[:, cols]), _dot(yb, wb_ref[:, cols]))

    def project(r, c):
        gate_a, gate_b, branch_a, branch_b = pre.pop((r, c))
        merged = jax.nn.sigmoid(gate_a) * branch_a + jax.nn.sigmoid(gate_b) * branch_b
        t = _dot(merged, wo_ref[c * FF_CHUNK:(c + 1) * FF_CHUNK, :])
        acc[r] = t if c == 0 else acc[r] + t
        if c == n_chunks - 1:
            rows = slice(r * SUB_ROWS, (r + 1) * SUB_ROWS)
            o_ref[rows, :] = _layer_norm(DEEPNORM_ALPHA * x_ref[rows, :] + acc.pop(r), gain, bias)

    for s in range(len(units) + 1):
        if s < len(units):
            branches(*units[s])
        if s >= 1:
            project(*units[s - 1])


def _merge(x2d, ya, yb, w_gate_a, w_gate_b, w_a, w_b, w_o, gain, bias):
    tokens = x2d.shape[0]
    rows = lambda width: pl.BlockSpec((ROW_TILE, width), lambda i: (i, 0))
    return pl.pallas_call(
        _merge_kernel,
        grid=(tokens // ROW_TILE,),
        in_specs=[
            rows(D_MODEL), rows(SB_WIDTH), rows(DIFF_WIDTH),
            _resident((D_MODEL, D_MODEL)), _resident((D_MODEL, D_MODEL)),
            _resident((SB_WIDTH, D_MODEL)), _resident((DIFF_WIDTH, D_MODEL)),
            _resident((D_MODEL, D_MODEL)), _resident((1, D_MODEL)), _resident((1, D_MODEL)),
        ],
        out_specs=rows(D_MODEL),
        out_shape=jax.ShapeDtypeStruct((tokens, D_MODEL), F32),
        compiler_params=pltpu.CompilerParams(dimension_semantics=("arbitrary",), vmem_limit_bytes=VMEM_LIMIT),
        name="gated_merge_layernorm",
    )(x2d, ya, yb, w_gate_a, w_gate_b, w_a, w_b, w_o, gain, bias)


def _mlp_kernel(x_ref, w1_ref, w2_ref, gain_ref, bias_ref, o_ref):
    gain, bias = gain_ref[...], bias_ref[...]
    n_chunks = D_FF // FF_CHUNK
    units = [(r, c) for r in range(ROW_TILE // SUB_ROWS) for c in range(n_chunks)]
    xb, hidden, acc = {}, {}, {}

    def up(r, c):
        if c == 0:
            xb[r] = x_ref[r * SUB_ROWS:(r + 1) * SUB_ROWS, :].astype(BF16)
        hidden[r, c] = _dot(xb[r], w1_ref[:, c * FF_CHUNK:(c + 1) * FF_CHUNK])

    def down(r, c):
        h = jnp.maximum(hidden.pop((r, c)), 0.0)
        t = _dot(h * h, w2_ref[c * FF_CHUNK:(c + 1) * FF_CHUNK, :])
        acc[r] = t if c == 0 else acc[r] + t
        if c == n_chunks - 1:
            rows = slice(r * SUB_ROWS, (r + 1) * SUB_ROWS)
            o_ref[rows, :] = _layer_norm(DEEPNORM_ALPHA * x_ref[rows, :] + acc.pop(r), gain, bias)

    for s in range(len(units) + 1):
        if s < len(units):
            up(*units[s])
        if s >= 1:
            down(*units[s - 1])


def _mlp(x2d, w1, w2, gain, bias):
    tokens = x2d.shape[0]
    rows = pl.BlockSpec((ROW_TILE, D_MODEL), lambda i: (i, 0))
    return pl.pallas_call(
        _mlp_kernel,
        grid=(tokens // ROW_TILE,),
        in_specs=[rows, _resident((D_MODEL, D_FF)), _resident((D_FF, D_MODEL)),
                  _resident((1, D_MODEL)), _resident((1, D_MODEL))],
        out_specs=rows,
        out_shape=jax.ShapeDtypeStruct((tokens, D_MODEL), F32),
        compiler_params=pltpu.CompilerParams(dimension_semantics=("arbitrary",), vmem_limit_bytes=VMEM_LIMIT),
        name="mlp_layernorm",
    )(x2d, w1, w2, gain, bias)


def _rope_tables(seq):
    inv_freq = ROPE_THETA ** (-jnp.arange(0, HEAD_DIM, 2, dtype=F32) / HEAD_DIM)
    ang = jnp.arange(seq, dtype=F32)[:, None] * inv_freq[None, :]
    cos, sin = jnp.cos(ang), jnp.sin(ang)
    groups = LANES // HEAD_DIM
    return jnp.tile(cos, (1, 2 * groups)), jnp.tile(jnp.concatenate([-sin, sin], axis=1), (1, groups))


def kernel(x, w_in, w_branch_a, w_branch_b, w_out, lambda_q1, lambda_k1, lambda_q2, lambda_k2, subln_gain,
           ln1_gain, ln1_bias, w_ff1, w_ff2, ln2_gain, ln2_bias):
    batch, seq, _ = x.shape
    assert seq % ROW_TILE == 0 and seq % ATT_TILE == 0 and w_in.shape[0] == DEPTH
    cos_t, sin_t = _rope_tables(seq)
    tri = jnp.asarray(np.tril(np.ones((ATT_TILE, ATT_TILE), np.float32)), BF16)
    x2d = x.reshape(batch * seq, D_MODEL)
    for l in range(DEPTH):
        later = (w_branch_a[l], w_branch_b[l], w_out[l], w_ff1[l], w_ff2[l])
        qkv_a, qkv_b, w_gates, (w_a, w_b, w_o, w_1, w_2) = _project(x2d, w_in[l], later, cos_t, sin_t, seq)
        ya = _stick_breaking(qkv_a, tri, batch, seq).reshape(batch * seq, SB_WIDTH)
        yb = _differential(qkv_b, lambda_q1[l][None], lambda_k1[l][None], lambda_q2[l][None], lambda_k2[l][None],
                           subln_gain[l][None], batch, seq).reshape(batch * seq, DIFF_WIDTH)
        x2d = _merge(x2d, ya, yb, *w_gates, w_a, w_b, w_o, ln1_gain[l][None], ln1_bias[l][None])
        x2d = _mlp(x2d, w_1, w_2, ln2_gain[l][None], ln2_bias[l][None])
    return x2d.reshape(batch, seq, D_MODEL)
```

```python
import functools
import math

import numpy as np
import jax
import jax.numpy as jnp
from jax import lax
from jax.experimental import pallas as pl
from jax.experimental.pallas import tpu as pltpu

D_MODEL = 1024
HEAD_DIM = 64
SB_HEADS = 8
DIFF_HEADS = 4
SB_WIDTH = SB_HEADS * HEAD_DIM
DIFF_WIDTH = DIFF_HEADS * 2 * HEAD_DIM
QKV_WIDTH = 3 * SB_WIDTH + 3 * DIFF_WIDTH
D_FF = 4 * D_MODEL
ROPE_THETA = 10000.0
LN_EPS = 1e-5
RMS_EPS = 1e-5
DEPTH = 1
DEEPNORM_ALPHA = (2.0 * DEPTH) ** 0.25
LAMBDA_INIT = 0.8 - 0.6 * math.exp(-0.3 * 0)
LOG2E = math.log2(math.e)
SOFTPLUS_CLAMP = 64.0
QK_SCALE = HEAD_DIM ** -0.5 * LOG2E

LANES = 128
BF16_SUBLANES = 16
BF16_BITS = 0xFFFF0000
ROW_TILE = 1024
SUB_ROWS = 256
ATT_TILE = 256
DIFF_HEADS_PER_STEP = 2
SB_PAIRS_PER_STEP = 2
FF_CHUNK = 512
VMEM_LIMIT = 56 * 1024 * 1024

F32 = jnp.float32
BF16 = jnp.bfloat16


def _dot(a, b):
    return lax.dot_general(a, b, (((1,), (0,)), ((), ())), preferred_element_type=F32)


def _dot_nt(a, b):
    return lax.dot_general(a, b, (((1,), (1,)), ((), ())), preferred_element_type=F32)


def _resident(shape):
    return pl.BlockSpec(shape, lambda *_: (0,) * len(shape), pipeline_mode=pl.Buffered(1))


def _row_reduce(elementwise, lane_reduce, pieces):
    chunks = [p[:, c:c + LANES] for p in pieces for c in range(0, p.shape[1], LANES)]
    return lane_reduce(functools.reduce(elementwise, chunks), axis=1, keepdims=True)


def _quadrants(top_left, bottom_left, bottom_right):
    top = jnp.concatenate([top_left, jnp.zeros_like(top_left)], axis=1)
    return jnp.concatenate([top, jnp.concatenate([bottom_left, bottom_right], axis=1)], axis=0)


def _layer_norm(r, gain, bias):
    mu = jnp.mean(r, axis=-1, keepdims=True)
    c = r - mu
    var = jnp.mean(c * c, axis=-1, keepdims=True)
    return c * lax.rsqrt(var + LN_EPS) * gain + bias


def _proj_kernel(x_ref, w_ref, cos_ref, sin_ref, *refs):
    n_cast = (len(refs) - 2) // 2
    cast_in, (oa_ref, ob_ref), cast_out = refs[:n_cast], refs[n_cast:n_cast + 2], refs[n_cast + 2:]
    for src, dst in zip(cast_in, cast_out):
        dst[...] = src[...].astype(BF16)

    xb = x_ref[...].astype(BF16)
    lane = lax.broadcasted_iota(jnp.int32, (ROW_TILE, LANES), 1)
    first_half = (lane % HEAD_DIM) < (HEAD_DIM // 2)
    pos = pl.multiple_of((pl.program_id(0) % (cos_ref.shape[0] // ROW_TILE)) * ROW_TILE, ROW_TILE)
    cos = cos_ref[pl.ds(pos, ROW_TILE), :]
    sin = sin_ref[pl.ds(pos, ROW_TILE), :]

    def rope(t):
        swapped = jnp.where(first_half, pltpu.roll(t, LANES - HEAD_DIM // 2, 1), pltpu.roll(t, HEAD_DIM // 2, 1))
        return t * cos + swapped * sin

    n_sec = SB_WIDTH // LANES
    for sec in range(QKV_WIDTH // SB_WIDTH):
        acc = _dot(xb, w_ref[:, sec * SB_WIDTH:(sec + 1) * SB_WIDTH].astype(BF16))
        out_ref = oa_ref if sec < 3 else ob_ref
        col0 = (sec % 3) * SB_WIDTH
        for g in range(n_sec):
            t = acc[:, g * LANES:(g + 1) * LANES]
            if sec in (3, 4):
                t = rope(t)
            if sec in (0, 3):
                t = t * QK_SCALE
            out_ref[:, col0 + g * LANES:col0 + (g + 1) * LANES] = t.astype(BF16)


def _project(x2d, w_in, later_weights, cos_t, sin_t, seq):
    tokens = x2d.shape[0]
    steps = tokens // ROW_TILE
    gate_blocks = [QKV_WIDTH // D_MODEL + c for c in range((w_in.shape[1] - QKV_WIDTH) // D_MODEL)]
    cast_arrays = [w_in] * len(gate_blocks) + list(later_weights)
    cast_in, cast_out, cast_shapes = [], [], []
    for n, w in enumerate(cast_arrays):
        rows = w.shape[0] // steps
        assert rows * steps == w.shape[0] and rows % BF16_SUBLANES == 0
        width = D_MODEL if n < len(gate_blocks) else w.shape[1]
        col = gate_blocks[n] if n < len(gate_blocks) else 0
        cast_in.append(pl.BlockSpec((rows, width), lambda i, col=col: (i, col)))
        cast_out.append(pl.BlockSpec((rows, width), lambda i: (i, 0)))
        cast_shapes.append(jax.ShapeDtypeStruct((w.shape[0], width), BF16))
    outs = pl.pallas_call(
        _proj_kernel,
        grid=(steps,),
        in_specs=[
            pl.BlockSpec((ROW_TILE, D_MODEL), lambda i: (i, 0)),
            pl.BlockSpec((D_MODEL, QKV_WIDTH), lambda i: (0, 0), pipeline_mode=pl.Buffered(1)),
            _resident((seq, LANES)), _resident((seq, LANES)),
        ] + cast_in,
        out_specs=[
            pl.BlockSpec((ROW_TILE, 3 * SB_WIDTH), lambda i: (i, 0)),
            pl.BlockSpec((ROW_TILE, 3 * DIFF_WIDTH), lambda i: (i, 0)),
        ] + cast_out,
        out_shape=[
            jax.ShapeDtypeStruct((tokens, 3 * SB_WIDTH), BF16),
            jax.ShapeDtypeStruct((tokens, 3 * DIFF_WIDTH), BF16),
        ] + cast_shapes,
        compiler_params=pltpu.CompilerParams(dimension_semantics=("arbitrary",), vmem_limit_bytes=VMEM_LIMIT),
        name="qkv_projection",
    )(x2d, w_in, cos_t, sin_t, *cast_arrays)
    return outs[0], outs[1], outs[2:2 + len(gate_blocks)], outs[2 + len(gate_blocks):]


def _sb_kernel(q_ref, k_ref, v_ref, tri_ref, o_ref, vh_ref):
    n_tiles = q_ref.shape[1] // ATT_TILE
    pairs = range(q_ref.shape[2] // LANES)
    half = ATT_TILE // 2
    tri = tri_ref[...]
    lane = lax.broadcasted_iota(jnp.int32, (ATT_TILE, LANES), 1)
    low = lane < HEAD_DIM
    row = lax.broadcasted_iota(jnp.int32, (half, half), 0)
    col = lax.broadcasted_iota(jnp.int32, (half, half), 1)
    strictly_causal = col < row

    def keys(ref, p, j):
        return ref[0, j * ATT_TILE:(j + 1) * ATT_TILE, p * LANES:(p + 1) * LANES]

    for p in pairs:
        for j in range(n_tiles):
            vt = keys(v_ref, p, j)
            zero = jnp.zeros_like(vt)
            vh_ref[2 * p, j * ATT_TILE:(j + 1) * ATT_TILE, :] = jnp.where(low, vt, zero)
            vh_ref[2 * p + 1, j * ATT_TILE:(j + 1) * ATT_TILE, :] = jnp.where(low, zero, vt)

    def softplus2(z2):
        return jnp.maximum(z2, jnp.log(1.0 + jnp.exp2(jnp.minimum(z2, SOFTPLUS_CLAMP))) * LOG2E)

    tiles = [(p, qi, j) for p in pairs for qi in range(n_tiles) for j in range(qi, -1, -1)]
    qh, scored, summed, acc, carry = {}, {}, {}, {}, {}

    def score(p, qi, j):
        if j == qi:
            q = keys(q_ref, p, qi)
            zero = jnp.zeros_like(q)
            qh[p] = (jnp.where(low, q, zero), jnp.where(low, zero, q))
        kt = keys(k_ref, p, j)
        out = []
        for h in range(2):
            z2 = _dot_nt(qh[p][h], kt)
            if j == qi:
                sp = _quadrants(jnp.where(strictly_causal, softplus2(z2[:half, :half]), 0.0), softplus2(z2[half:, :half]),
                                jnp.where(strictly_causal, softplus2(z2[half:, half:]), 0.0))
            else:
                sp = softplus2(z2)
            out.append((z2, sp))
        scored[p, qi, j] = out

    def in_tile_sums(p, qi, j):
        items = scored.pop((p, qi, j))
        his = [lax.bitcast_convert_type(lax.bitcast_convert_type(sp, jnp.uint32) & jnp.uint32(BF16_BITS), F32)
               for _, sp in items]
        high = [_dot(hi, tri) for hi in his]
        rest = [_dot(sp - hi, tri) for (_, sp), hi in zip(items, his)]
        summed[p, qi, j] = [(z2, a + b) for (z2, _), a, b in zip(items, high, rest)]

    def value_product(p, qi, j):
        for h, (z2, csum) in enumerate(summed.pop((p, qi, j))):
            x = z2 - csum
            if j == qi:
                a = _quadrants(jnp.where(strictly_causal, jnp.exp2(x[:half, :half]), 0.0), jnp.exp2(x[half:, :half]),
                               jnp.where(strictly_causal, jnp.exp2(x[half:, half:]), 0.0))
            else:
                a = jnp.exp2(x - jnp.concatenate([carry[p, h], carry[p, h]], axis=1))
            if j > 0:
                total = jnp.broadcast_to(csum[:, 0:1], (ATT_TILE, LANES))
                carry[p, h] = total if j == qi else carry[p, h] + total
            t = _dot(a, vh_ref[2 * p + h, j * ATT_TILE:(j + 1) * ATT_TILE, :])
            acc[p] = t if (j == qi and h == 0) else acc[p] + t
        if j == 0:
            o_ref[0, qi * ATT_TILE:(qi + 1) * ATT_TILE, p * LANES:(p + 1) * LANES] = acc.pop(p).astype(BF16)

    for s in range(len(tiles) + 2):
        if s < len(tiles):
            score(*tiles[s])
        if 1 <= s <= len(tiles):
            in_tile_sums(*tiles[s - 1])
        if s >= 2:
            value_product(*tiles[s - 2])


def _stick_breaking(qkv, tri, batch, seq):
    qkv = qkv.reshape(batch, seq, 3 * SB_WIDTH)
    groups = SB_WIDTH // (SB_PAIRS_PER_STEP * LANES)
    blk = (1, seq, SB_PAIRS_PER_STEP * LANES)
    return pl.pallas_call(
        _sb_kernel,
        grid=(batch, groups),
        in_specs=[
            pl.BlockSpec(blk, lambda b, p: (b, 0, p)),
            pl.BlockSpec(blk, lambda b, p: (b, 0, groups + p)),
            pl.BlockSpec(blk, lambda b, p: (b, 0, 2 * groups + p)),
            _resident((ATT_TILE, ATT_TILE)),
        ],
        out_specs=pl.BlockSpec(blk, lambda b, p: (b, 0, p)),
        out_shape=jax.ShapeDtypeStruct((batch, seq, SB_WIDTH), BF16),
        scratch_shapes=[pltpu.VMEM((2 * SB_PAIRS_PER_STEP, seq, LANES), BF16)],
        compiler_params=pltpu.CompilerParams(dimension_semantics=("arbitrary", "arbitrary"),
                                             vmem_limit_bytes=VMEM_LIMIT),
        name="stick_breaking_attention",
    )(qkv, qkv, qkv, tri)


def _diff_kernel(q_ref, k_ref, v_ref, lq1_ref, lk1_ref, lq2_ref, lk2_ref, gain_ref, o_ref):
    n_tiles = q_ref.shape[1] // ATT_TILE
    half = ATT_TILE // 2
    lam = (jnp.exp(jnp.sum(lq1_ref[...] * lk1_ref[...], keepdims=True))
           - jnp.exp(jnp.sum(lq2_ref[...] * lk2_ref[...], keepdims=True)) + LAMBDA_INIT)
    lane = lax.broadcasted_iota(jnp.int32, (ATT_TILE, LANES), 1)
    low = lane < HEAD_DIM
    row = lax.broadcasted_iota(jnp.int32, (half, half), 0)
    col = lax.broadcasted_iota(jnp.int32, (half, half), 1)
    causal = col <= row
    gain = gain_ref[...]

    def keys(ref, head, j):
        return ref[0, j * ATT_TILE:(j + 1) * ATT_TILE, head * LANES:(head + 1) * LANES]

    def scores(head, qi):
        q = keys(q_ref, head, qi)
        zero = jnp.zeros_like(q)
        return [[_dot_nt(qh, keys(k_ref, head, j)) for j in range(qi + 1)]
                for qh in (jnp.where(low, q, zero), jnp.where(low, zero, q))]

    def weigh(head, qi, zs):
        weights, row_scale = [], []
        for top in (True, False):
            rows = slice(0, half) if top else slice(half, ATT_TILE)
            e, denom = [], []
            for z in zs:
                pieces = [zt[rows, :] for zt in z[:qi]]
                diag = z[qi]
                if top:
                    pieces.append(jnp.where(causal, diag[:half, :half], -jnp.inf))
                else:
                    pieces += [diag[half:, :half], jnp.where(causal, diag[half:, half:], -jnp.inf)]
                m = _row_reduce(jnp.maximum, jnp.max, pieces)
                e.append([jnp.exp2(p - m) for p in pieces])
                denom.append(_row_reduce(jnp.add, jnp.sum, e[-1]))
            rho = lam * denom[0] / denom[1]
            weights.append([e1 - rho * e2 for e1, e2 in zip(*e)])
            row_scale.append(1.0 / denom[0])
        y = None
        for j in range(qi + 1):
            if j < qi:
                w = jnp.concatenate([weights[0][j], weights[1][j]], axis=0)
            else:
                w = _quadrants(weights[0][qi], weights[1][qi], weights[1][qi + 1])
            t = _dot(w, keys(v_ref, head, j))
            y = t if y is None else y + t
        y = y * jnp.concatenate(row_scale, axis=0)
        y = y * lax.rsqrt(jnp.mean(y * y, axis=-1, keepdims=True) + RMS_EPS) * gain
        o_ref[0, qi * ATT_TILE:(qi + 1) * ATT_TILE, head * LANES:(head + 1) * LANES] = (
            y * (1.0 - LAMBDA_INIT)).astype(BF16)

    heads = range(q_ref.shape[2] // LANES)
    pending = [scores(head, 0) for head in heads]
    for qi in range(n_tiles):
        following = [scores(head, qi + 1) if qi + 1 < n_tiles else None for head in heads]
        for head in heads:
            weigh(head, qi, pending[head])
        pending = following


def _differential(qkv, lq1, lk1, lq2, lk2, gain, batch, seq):
    qkv = qkv.reshape(batch, seq, 3 * DIFF_WIDTH)
    blk = (1, seq, DIFF_HEADS_PER_STEP * LANES)
    groups = DIFF_HEADS // DIFF_HEADS_PER_STEP
    return pl.pallas_call(
        _diff_kernel,
        grid=(batch, groups),
        in_specs=[
            pl.BlockSpec(blk, lambda b, h: (b, 0, h)),
            pl.BlockSpec(blk, lambda b, h: (b, 0, groups + h)),
            pl.BlockSpec(blk, lambda b, h: (b, 0, 2 * groups + h)),
            _resident((1, HEAD_DIM)), _resident((1, HEAD_DIM)), _resident((1, HEAD_DIM)), _resident((1, HEAD_DIM)),
            _resident((1, LANES)),
        ],
        out_specs=pl.BlockSpec(blk, lambda b, h: (b, 0, h)),
        out_shape=jax.ShapeDtypeStruct((batch, seq, DIFF_WIDTH), BF16),
        compiler_params=pltpu.CompilerParams(dimension_semantics=("arbitrary", "arbitrary"),
                                             vmem_limit_bytes=VMEM_LIMIT),
        name="differential_attention",
    )(qkv, qkv, qkv, lq1, lk1, lq2, lk2, gain)


def _merge_kernel(x_ref, ya_ref, yb_ref, wga_ref, wgb_ref, wa_ref, wb_ref, wo_ref, gain_ref, bias_ref, *refs):
    n_cast = (len(refs) - 1) // 2
    cast_in, o_ref, cast_out = refs[:n_cast], refs[n_cast], refs[n_cast + 1:]
    for src, dst in zip(cast_in, cast_out):
        dst[...] = src[...].astype(BF16)
    gain, bias = gain_ref[...], bias_ref[...]
    n_chunks = D_MODEL // FF_CHUNK
    units = [(r, c) for r in range(ROW_TILE // SUB_ROWS) for c in range(n_chunks)]
    lhs, pre, acc = {}, {}, {}

    def branches(r, c):
        rows = slice(r * SUB_ROWS, (r + 1) * SUB_ROWS)
        cols = slice(c * FF_CHUNK, (c + 1) * FF_CHUNK)
        if c == 0:
            lhs[r] = (x_ref[rows, :].astype(BF16), ya_ref[rows, :], yb_ref[rows, :])
        xb, ya, yb = lhs[r]
        pre[r, c] = (_dot(xb, wga_ref[:, cols]), _dot(xb, wgb_ref[:, cols]),
                     _dot(ya, wa_ref[:, cols]), _dot(yb, wb_ref[:, cols]))

    def project(r, c):
        gate_a, gate_b, branch_a, branch_b = pre.pop((r, c))
        merged = jax.nn.sigmoid(gate_a) * branch_a + jax.nn.sigmoid(gate_b) * branch_b
        t = _dot(merged, wo_ref[c * FF_CHUNK:(c + 1) * FF_CHUNK, :])
        acc[r] = t if c == 0 else acc[r] + t
        if c == n_chunks - 1:
            rows = slice(r * SUB_ROWS, (r + 1) * SUB_ROWS)
            o_ref[rows, :] = _layer_norm(DEEPNORM_ALPHA * x_ref[rows, :] + acc.pop(r), gain, bias)

    for s in range(len(units) + 1):
        if s < len(units):
            branches(*units[s])
        if s >= 1:
            project(*units[s - 1])


def _merge(x2d, ya, yb, w_gate_a, w_gate_b, w_a, w_b, w_o, gain, bias, mlp_weights):
    tokens = x2d.shape[0]
    steps = tokens // ROW_TILE
    rows = lambda width: pl.BlockSpec((ROW_TILE, width), lambda i: (i, 0))
    cast_specs = []
    for w in mlp_weights:
        block_rows = w.shape[0] // steps
        assert block_rows * steps == w.shape[0] and block_rows % BF16_SUBLANES == 0
        cast_specs.append(pl.BlockSpec((block_rows, w.shape[1]), lambda i: (i, 0)))
    outs = pl.pallas_call(
        _merge_kernel,
        grid=(steps,),
        in_specs=[
            rows(D_MODEL), rows(SB_WIDTH), rows(DIFF_WIDTH),
            _resident((D_MODEL, D_MODEL)), _resident((D_MODEL, D_MODEL)),
            _resident((SB_WIDTH, D_MODEL)), _resident((DIFF_WIDTH, D_MODEL)),
            _resident((D_MODEL, D_MODEL)), _resident((1, D_MODEL)), _resident((1, D_MODEL)),
        ] + cast_specs,
        out_specs=[rows(D_MODEL)] + cast_specs,
        out_shape=[jax.ShapeDtypeStruct((tokens, D_MODEL), F32)]
        + [jax.ShapeDtypeStruct(w.shape, BF16) for w in mlp_weights],
        compiler_params=pltpu.CompilerParams(dimension_semantics=("arbitrary",), vmem_limit_bytes=VMEM_LIMIT),
        name="gated_merge_layernorm",
    )(x2d, ya, yb, w_gate_a, w_gate_b, w_a, w_b, w_o, gain, bias, *mlp_weights)
    return outs[0], outs[1:]


def _mlp_kernel(x_ref, w1_ref, w2_ref, gain_ref, bias_ref, o_ref):
    gain, bias = gain_ref[...], bias_ref[...]
    n_chunks = D_FF // FF_CHUNK
    units = [(r, c) for r in range(ROW_TILE // SUB_ROWS) for c in range(n_chunks)]
    xb, hidden, acc = {}, {}, {}

    def up(r, c):
        if c == 0:
            xb[r] = x_ref[r * SUB_ROWS:(r + 1) * SUB_ROWS, :].astype(BF16)
        hidden[r, c] = _dot(xb[r], w1_ref[:, c * FF_CHUNK:(c + 1) * FF_CHUNK])

    def down(r, c):
        h = jnp.maximum(hidden.pop((r, c)), 0.0)
        t = _dot(h * h, w2_ref[c * FF_CHUNK:(c + 1) * FF_CHUNK, :])
        acc[r] = t if c == 0 else acc[r] + t
        if c == n_chunks - 1:
            rows = slice(r * SUB_ROWS, (r + 1) * SUB_ROWS)
            o_ref[rows, :] = _layer_norm(DEEPNORM_ALPHA * x_ref[rows, :] + acc.pop(r), gain, bias)

    for s in range(len(units) + 1):
        if s < len(units):
            up(*units[s])
        if s >= 1:
            down(*units[s - 1])


def _mlp(x2d, w1, w2, gain, bias):
    tokens = x2d.shape[0]
    rows = pl.BlockSpec((ROW_TILE, D_MODEL), lambda i: (i, 0))
    return pl.pallas_call(
        _mlp_kernel,
        grid=(tokens // ROW_TILE,),
        in_specs=[rows, _resident((D_MODEL, D_FF)), _resident((D_FF, D_MODEL)),
                  _resident((1, D_MODEL)), _resident((1, D_MODEL))],
        out_specs=rows,
        out_shape=jax.ShapeDtypeStruct((tokens, D_MODEL), F32),
        compiler_params=pltpu.CompilerParams(dimension_semantics=("arbitrary",), vmem_limit_bytes=VMEM_LIMIT),
        name="mlp_layernorm",
    )(x2d, w1, w2, gain, bias)


def _rope_tables(seq):
    inv_freq = ROPE_THETA ** (-jnp.arange(0, HEAD_DIM, 2, dtype=F32) / HEAD_DIM)
    ang = jnp.arange(seq, dtype=F32)[:, None] * inv_freq[None, :]
    cos, sin = jnp.cos(ang), jnp.sin(ang)
    groups = LANES // HEAD_DIM
    return jnp.tile(cos, (1, 2 * groups)), jnp.tile(jnp.concatenate([-sin, sin], axis=1), (1, groups))


def kernel(x, w_in, w_branch_a, w_branch_b, w_out, lambda_q1, lambda_k1, lambda_q2, lambda_k2, subln_gain,
           ln1_gain, ln1_bias, w_ff1, w_ff2, ln2_gain, ln2_bias):
    batch, seq, _ = x.shape
    assert seq % ROW_TILE == 0 and seq % ATT_TILE == 0 and w_in.shape[0] == DEPTH
    cos_t, sin_t = _rope_tables(seq)
    tri = jnp.asarray(np.tril(np.ones((ATT_TILE, ATT_TILE), np.float32)), BF16)
    x2d = x.reshape(batch * seq, D_MODEL)
    for l in range(DEPTH):
        later = (w_branch_a[l], w_branch_b[l], w_out[l])
        qkv_a, qkv_b, w_gates, (w_a, w_b, w_o) = _project(x2d, w_in[l], later, cos_t, sin_t, seq)
        ya = _stick_breaking(qkv_a, tri, batch, seq).reshape(batch * seq, SB_WIDTH)
        yb = _differential(qkv_b, lambda_q1[l][None], lambda_k1[l][None], lambda_q2[l][None], lambda_k2[l][None],
                           subln_gain[l][None], batch, seq).reshape(batch * seq, DIFF_WIDTH)
        x2d, (w_1, w_2) = _merge(x2d, ya, yb, *w_gates, w_a, w_b, w_o, ln1_gain[l][None], ln1_bias[l][None],
                                 (w_ff1[l], w_ff2[l]))
        x2d = _mlp(x2d, w_1, w_2, ln2_gain[l][None], ln2_bias[l][None])
    return x2d.reshape(batch, seq, D_MODEL)
```
